```python
import math
import jax
import jax.numpy as jnp
from jax import lax
import numpy as np

D_MODEL = 2048
BATCH = 32
SEQ = 256
DEPTH = 2
DEC_BATCH = 2
DEC_SEQ = 1024
PAST_LEN = 256

GRID_W = 64
Q_BLOCK = 128
ROPE_THETA = 10000.0
EPS = 1e-6
N_MOD = 6

HY_CH = 512
HY_ORDER = 2
HY_SHORT = 3
HY_BANDS = 16
HY_EMB = 1 + 2 * HY_BANDS
HY_FFN = 64
HY_DECAY_SHORT_PCT = 0.3
HY_DECAY_LONG_PCT = 1.5
HY_DECAY_TARGET = 1e-2

DF_HEADS = 4
DF_QK = 64
DF_V = 2 * DF_QK

GQ_HEADS = 8
GQ_KV = 2
GQ_DIM = 128

N_BRANCH = 3

N_EXPERTS = 32
TOP_K = 4
D_FF = D_MODEL
SWIGLU_LIMIT = 7.0
SWIGLU_ALPHA = 1.702

IN_SPLITS = (HY_CH * (HY_ORDER + 1), DF_HEADS * 2 * DF_QK, DF_HEADS * 2 * DF_QK, DF_HEADS * DF_V,
             GQ_HEADS * GQ_DIM, GQ_KV * GQ_DIM, GQ_KV * GQ_DIM, N_BRANCH * D_MODEL)
D_IN = sum(IN_SPLITS)
F32 = jnp.float32

kernel_name = 'hybrid_dit_hyena_diffattn_gqa_moe_step'


def rmsnorm(x, g):
    xf = x.astype(F32)
    y = xf * lax.rsqrt(jnp.mean(xf * xf, axis=-1, keepdims=True) + EPS)
    return (y * g.astype(F32)).astype(x.dtype)


def adaln(cond, w, b):
    m = jax.nn.silu(cond) @ w + b
    return jnp.split(m[..., None, :], N_MOD, axis=-1)


def modulate(x, g, shift, scale):
    return rmsnorm(x, g) * (1 + scale) + shift


def grid_positions(L):
    rows = L // GRID_W
    t = jnp.arange(rows * GRID_W, dtype=jnp.int32)
    return t // GRID_W, t % GRID_W


def rope_1d(x, pos):
    half = x.shape[-1] // 2
    inv = ROPE_THETA ** (-jnp.arange(half, dtype=F32) / half)
    ang = pos.astype(F32)[:, None] * inv
    ang = ang.reshape((ang.shape[0],) + (1,) * (x.ndim - 3) + (half,))
    cos, sin = jnp.cos(ang), jnp.sin(ang)
    x1, x2 = x[..., :half].astype(F32), x[..., half:].astype(F32)
    return jnp.concatenate([x1 * cos - x2 * sin, x2 * cos + x1 * sin], -1).astype(x.dtype)


def rope_2d(x, row, col):
    d = x.shape[-1] // 2
    return jnp.concatenate([rope_1d(x[..., :d], row), rope_1d(x[..., d:], col)], -1)


def short_conv(u, w, b):
    y = lax.conv_general_dilated(u, w[:, None, :].astype(u.dtype), window_strides=(1,),
                                 padding=((HY_SHORT // 2, HY_SHORT // 2),),
                                 dimension_numbers=('NWC', 'WIO', 'NWC'),
                                 feature_group_count=u.shape[-1])
    return y + b.astype(u.dtype)


def hyena_filter_spectra(L, w1, b1, w2, b2, w3, freq):
    t = jnp.linspace(0.0, 1.0, L, dtype=F32)[:, None]
    bands = jnp.linspace(1e-4, HY_BANDS - 1, HY_BANDS, dtype=F32)
    wpos = (2 * math.pi / L) * jnp.arange(L, dtype=F32)[:, None] * bands
    z = jnp.concatenate([t, jnp.cos(wpos), -jnp.sin(wpos)], -1)
    fr = freq.astype(F32)
    h = jnp.sin(fr * (z @ w1.astype(F32) + b1.astype(F32)))
    h = jnp.sin(fr * (h @ w2.astype(F32) + b2.astype(F32)))
    h = (h @ w3.astype(F32)).reshape(L, HY_ORDER, 2, HY_CH)
    max_decay = math.log(HY_DECAY_TARGET) / HY_DECAY_SHORT_PCT
    min_decay = math.log(HY_DECAY_TARGET) / HY_DECAY_LONG_PCT
    deltas = jnp.linspace(min_decay, max_decay, HY_CH, dtype=F32)
    h = h * jnp.exp(-t[:, :, None, None] * jnp.abs(deltas))
    fwd, bwd = h[:, :, 0], h[:, :, 1]
    k = jnp.concatenate([fwd, jnp.zeros((1, HY_ORDER, HY_CH), F32), jnp.flip(bwd[1:], 0)], 0)
    return jnp.fft.rfft(k, axis=0)


def fft_long_conv(u, kf):
    L = u.shape[1]
    U = jnp.fft.rfft(u, n=2 * L, axis=1)
    return jnp.fft.irfft(U * kf[None], n=2 * L, axis=1)[:, :L]


def hyena_mixer(u, lp):
    L = u.shape[1]
    u = short_conv(u, lp['hy_conv_w'], lp['hy_conv_b'])
    x1, x2, v = jnp.split(u, HY_ORDER + 1, axis=-1)
    kf = hyena_filter_spectra(L, lp['hy_f_w1'], lp['hy_f_b1'], lp['hy_f_w2'], lp['hy_f_b2'],
                              lp['hy_f_w3'], lp['hy_freq'])
    skip = lp['hy_skip'].astype(F32)
    z = v.astype(F32)
    for n, xn in enumerate((x1, x2)):
        z = xn.astype(F32) * (fft_long_conv(z, kf[:, n]) + skip[n] * z)
    return z.astype(u.dtype)


def _to_blocks(x):
    B, L = x.shape[0], x.shape[1]
    return jnp.moveaxis(x.reshape((B, L // Q_BLOCK, Q_BLOCK) + x.shape[2:]), 1, 0)


def _from_blocks(y):
    y = jnp.moveaxis(y, 0, 1)
    return y.reshape((y.shape[0], y.shape[1] * y.shape[2]) + y.shape[3:])


def diff_attention(q, k, v, lam):
    scale = DF_QK ** -0.5

    def block(qb):
        s = jnp.einsum('bqhcd,bkhcd->bhcqk', qb, k).astype(F32) * scale
        p = jax.nn.softmax(s, axis=-1)
        w = p[:, :, 0] - lam * p[:, :, 1]
        return jnp.einsum('bhqk,bkhd->bqhd', w.astype(v.dtype), v)

    return _from_blocks(lax.map(block, _to_blocks(q)))


def gqa_attention(q, k, v):
    scale = GQ_DIM ** -0.5

    def block(qb):
        s = jnp.einsum('bqgrd,bkgd->bgrqk', qb, k).astype(F32) * scale
        p = jax.nn.softmax(s, axis=-1).astype(v.dtype)
        return jnp.einsum('bgrqk,bkgd->bqgrd', p, v)

    return _from_blocks(lax.map(block, _to_blocks(q)))


def token_mixer(h, lp, layer_idx, ctx):
    B, L, _ = h.shape
    split_at = np.cumsum(IN_SPLITS)[:-1]
    hy_u, dq, dk, dv, gq, gk, gv, gates = jnp.split(h @ lp['w_in'], split_at, axis=-1)
    dq = rmsnorm(dq.reshape(B, L, DF_HEADS, 2, DF_QK), lp['df_qn_g'])
    dk = rmsnorm(dk.reshape(B, L, DF_HEADS, 2, DF_QK), lp['df_kn_g'])
    dv = dv.reshape(B, L, DF_HEADS, DF_V)
    gq = rmsnorm(gq.reshape(B, L, GQ_HEADS, GQ_DIM), lp['gq_qn_g'])
    gk = rmsnorm(gk.reshape(B, L, GQ_KV, GQ_DIM), lp['gq_kn_g'])
    gv = gv.reshape(B, L, GQ_KV, GQ_DIM)
    own = (dk, dv, gk, gv)
    if ctx is None:
        kd, vd, kg, vg = dk, dv, gk, gv
    else:
        row, col = grid_positions(L)
        dq, dk = rope_2d(dq, row, col), rope_2d(dk, row, col)
        gq, gk = rope_2d(gq, row, col), rope_2d(gk, row, col)
        cdk, cdv, cgk, cgv = ctx
        kd = jnp.concatenate([dk, cdk.astype(dk.dtype)], axis=1)
        vd = jnp.concatenate([dv, cdv.astype(dv.dtype)], axis=1)
        kg = jnp.concatenate([gk, cgk.astype(gk.dtype)], axis=1)
        vg = jnp.concatenate([gv, cgv.astype(gv.dtype)], axis=1)
    y_a = hyena_mixer(hy_u, lp)
    lam_init = 0.8 - 0.6 * math.exp(-0.3 * layer_idx)
    lam = (jnp.exp(jnp.sum(lp['df_lq1'].astype(F32) * lp['df_lk1'].astype(F32)))
           - jnp.exp(jnp.sum(lp['df_lq2'].astype(F32) * lp['df_lk2'].astype(F32))) + lam_init)
    y_b = rmsnorm(diff_attention(dq, kd, vd, lam), lp['df_subln_g']) * (1 - lam_init)
    y_c = gqa_attention(gq.reshape(B, L, GQ_KV, GQ_HEADS // GQ_KV, GQ_DIM), kg, vg)
    g = jax.nn.sigmoid(gates.reshape(B, L, N_BRANCH, D_MODEL))
    merged = (g[:, :, 0] * (y_a @ lp['w_hy_o'])
              + g[:, :, 1] * (y_b.reshape(B, L, -1) @ lp['w_df_o'])
              + g[:, :, 2] * (y_c.reshape(B, L, -1) @ lp['w_gq_o']))
    return merged @ lp['w_out'], own


def moe(h, lp, w_gu, b_gu, w_dn, b_dn, layer_idx):
    B, L, D = h.shape
    t = h.reshape(B * L, D)
    logits = (t @ lp['router_w'] + lp['router_b']).astype(F32)
    vals, idx = lax.top_k(logits, TOP_K)
    wts = jax.nn.softmax(vals, axis=-1)
    combine = jnp.sum(jax.nn.one_hot(idx, N_EXPERTS, dtype=F32) * wts[..., None], axis=1)
    out = jnp.zeros((B * L, D), F32)
    for e in range(N_EXPERTS):
        gu = t @ w_gu[layer_idx, e] + b_gu[layer_idx, e]
        gate = jnp.minimum(gu[:, :D_FF], SWIGLU_LIMIT)
        up = jnp.clip(gu[:, D_FF:], -SWIGLU_LIMIT, SWIGLU_LIMIT)
        act = gate * jax.nn.sigmoid(SWIGLU_ALPHA * gate) * (up + 1)
        y = act @ w_dn[layer_idx, e] + b_dn[layer_idx, e]
        out = out + combine[:, e:e + 1] * y.astype(F32)
    return out.astype(h.dtype).reshape(B, L, D)


def trunk_layer(x, cond, lp, experts, layer_idx, ctx):
    sh1, sc1, g1, sh2, sc2, g2 = adaln(cond, lp['w_mod'], lp['b_mod'])
    mix, own = token_mixer(modulate(x, lp['norm1_g'], sh1, sc1), lp, layer_idx, ctx)
    x = x + g1 * mix
    w_gu, b_gu, w_dn, b_dn = experts
    x = x + g2 * moe(modulate(x, lp['norm2_g'], sh2, sc2), lp, w_gu, b_gu, w_dn, b_dn, layer_idx)
    return x, own


def setup_inputs(seed: int = 0) -> dict:
    key = jax.random.key(seed)
    ks = iter(jax.random.split(key, 48))

    def nrm(shape, scale=1.0):
        return scale * jax.random.normal(next(ks), shape, F32)

    def gain(shape):
        return 1.0 + nrm(shape, 0.05)

    D = D_MODEL
    return {
        'x_prompt': nrm((BATCH, SEQ, D)),
        'x_sample': nrm((DEC_BATCH, DEC_SEQ, D)),
        'cache_diff_k': nrm((DEC_BATCH, DEPTH, PAST_LEN, DF_HEADS, 2, DF_QK)),
        'cache_diff_v': nrm((DEC_BATCH, DEPTH, PAST_LEN, DF_HEADS, DF_V)),
        'cache_gqa_k': nrm((DEC_BATCH, DEPTH, PAST_LEN, GQ_KV, GQ_DIM)),
        'cache_gqa_v': nrm((DEC_BATCH, DEPTH, PAST_LEN, GQ_KV, GQ_DIM)),
        'c': nrm((DEC_BATCH, D)),
        'c_ctx': nrm((D,)),
        'w_mod': nrm((DEPTH, D, N_MOD * D), 0.5 * D ** -0.5),
        'b_mod': nrm((DEPTH, N_MOD * D), 0.02),
        'norm1_g': gain((DEPTH, D)),
        'norm2_g': gain((DEPTH, D)),
        'w_in': nrm((DEPTH, D, D_IN), D ** -0.5),
        'hy_conv_w': nrm((DEPTH, HY_SHORT, HY_CH * (HY_ORDER + 1)), HY_SHORT ** -0.5),
        'hy_conv_b': nrm((DEPTH, HY_CH * (HY_ORDER + 1)), 0.02),
        'hy_f_w1': nrm((DEPTH, HY_EMB, HY_FFN), HY_EMB ** -0.5),
        'hy_f_b1': nrm((DEPTH, HY_FFN), 0.02),
        'hy_f_w2': nrm((DEPTH, HY_FFN, HY_FFN), HY_FFN ** -0.5),
        'hy_f_b2': nrm((DEPTH, HY_FFN), 0.02),
        'hy_f_w3': nrm((DEPTH, HY_FFN, HY_ORDER * 2 * HY_CH), 0.05 * HY_FFN ** -0.5),
        'hy_freq': gain((DEPTH, HY_FFN)),
        'hy_skip': nrm((DEPTH, HY_ORDER, HY_CH), 0.5),
        'df_qn_g': gain((DEPTH, DF_QK)),
        'df_kn_g': gain((DEPTH, DF_QK)),
        'df_lq1': nrm((DEPTH, DF_QK), 0.1),
        'df_lk1': nrm((DEPTH, DF_QK), 0.1),
        'df_lq2': nrm((DEPTH, DF_QK), 0.1),
        'df_lk2': nrm((DEPTH, DF_QK), 0.1),
        'df_subln_g': gain((DEPTH, DF_V)),
        'gq_qn_g': gain((DEPTH, GQ_DIM)),
        'gq_kn_g': gain((DEPTH, GQ_DIM)),
        'w_hy_o': nrm((DEPTH, HY_CH, D), HY_CH ** -0.5),
        'w_df_o': nrm((DEPTH, DF_HEADS * DF_V, D), (DF_HEADS * DF_V) ** -0.5),
        'w_gq_o': nrm((DEPTH, GQ_HEADS * GQ_DIM, D), (GQ_HEADS * GQ_DIM) ** -0.5),
        'w_out': nrm((DEPTH, D, D), D ** -0.5),
        'router_w': nrm((DEPTH, D, N_EXPERTS), D ** -0.5),
        'router_b': nrm((DEPTH, N_EXPERTS), 0.01),
        'w_gu': nrm((DEPTH, N_EXPERTS, D, 2 * D_FF), D ** -0.5),
        'b_gu': nrm((DEPTH, N_EXPERTS, 2 * D_FF), 0.02),
        'w_dn': nrm((DEPTH, N_EXPERTS, D_FF, D), D_FF ** -0.5),
        'b_dn': nrm((DEPTH, N_EXPERTS, D), 0.02),
    }


def reference(x_prompt, x_sample, cache_diff_k, cache_diff_v, cache_gqa_k, cache_gqa_v, c, c_ctx,
              w_mod, b_mod, norm1_g, norm2_g, w_in, hy_conv_w, hy_conv_b, hy_f_w1, hy_f_b1,
              hy_f_w2, hy_f_b2, hy_f_w3, hy_freq, hy_skip, df_qn_g, df_kn_g, df_lq1, df_lk1,
              df_lq2, df_lk2, df_subln_g, gq_qn_g, gq_kn_g, w_hy_o, w_df_o, w_gq_o, w_out,
              router_w, router_b, w_gu, b_gu, w_dn, b_dn):
    experts = (w_gu, b_gu, w_dn, b_dn)
    y_prompt, y_sample = x_prompt, x_sample
    new_dk, new_dv, new_gk, new_gv = [], [], [], []
    for l in range(DEPTH):
        lp = dict(w_mod=w_mod[l], b_mod=b_mod[l], norm1_g=norm1_g[l], norm2_g=norm2_g[l],
                  w_in=w_in[l], hy_conv_w=hy_conv_w[l], hy_conv_b=hy_conv_b[l],
                  hy_f_w1=hy_f_w1[l], hy_f_b1=hy_f_b1[l], hy_f_w2=hy_f_w2[l], hy_f_b2=hy_f_b2[l],
                  hy_f_w3=hy_f_w3[l], hy_freq=hy_freq[l], hy_skip=hy_skip[l],
                  df_qn_g=df_qn_g[l], df_kn_g=df_kn_g[l], df_lq1=df_lq1[l], df_lk1=df_lk1[l],
                  df_lq2=df_lq2[l], df_lk2=df_lk2[l], df_subln_g=df_subln_g[l],
                  gq_qn_g=gq_qn_g[l], gq_kn_g=gq_kn_g[l], w_hy_o=w_hy_o[l], w_df_o=w_df_o[l],
                  w_gq_o=w_gq_o[l], w_out=w_out[l], router_w=router_w[l], router_b=router_b[l])
        y_prompt, (dk, dv, gk, gv) = trunk_layer(y_prompt, c_ctx, lp, experts, l, None)
        new_dk.append(dk)
        new_dv.append(dv)
        new_gk.append(gk)
        new_gv.append(gv)
        ctx = (cache_diff_k[:, l], cache_diff_v[:, l], cache_gqa_k[:, l], cache_gqa_v[:, l])
        y_sample, _ = trunk_layer(y_sample, c, lp, experts, l, ctx)
    return (y_prompt, y_sample, jnp.stack(new_dk, axis=1), jnp.stack(new_dv, axis=1),
            jnp.stack(new_gk, axis=1), jnp.stack(new_gv, axis=1))
```

```python
import functools
import math

import jax
import jax.numpy as jnp
from jax import lax
from jax.experimental import pallas as pl
from jax.experimental.pallas import tpu as pltpu

F32 = jnp.float32
BF16 = jnp.bfloat16

D = 2048
BATCH, SEQ = 32, 256
DEPTH = 2
DEC_BATCH, DEC_SEQ = 2, 1024
PAST = 256
GRID_W = 64
ROPE_THETA = 10000.0
EPS = 1e-6
N_MOD = 6
HY_CH = 512
HY_BANDS = 16
HY_FFN = 64
DF_HEADS, DF_QK, DF_V = 4, 64, 128
GQ_HEADS, GQ_KV, GQ_DIM = 8, 2, 128
N_EXPERTS, TOP_K = 32, 4
D_FF = D
SWIGLU_LIMIT = 7.0
SWIGLU_ALPHA = 1.702

T_CTX = BATCH * SEQ
T_LAT = DEC_BATCH * DEC_SEQ
T = T_CTX + T_LAT
N_ASSIGN = T * TOP_K

C_HY = 0
C_DQ = 3 * HY_CH
C_DK = C_DQ + 512
C_DV = C_DK + 512
C_GQ = C_DV + 512
C_GK = C_GQ + 1024
C_GV = C_GK + 256
C_GATE = C_GV + 256
D_IN = C_GATE + 3 * D

LANE = 128
VMEM_LIMIT = 56 * 1024 * 1024

TM_TOK = 512
TM_PROJ = 1024
TN_PROJ = 512
QC = 256
TM_MOE = 512
TN_GU = 512
TN_DN = 512
NT_MOE = N_ASSIGN // TM_MOE + N_EXPERTS
NPAD = NT_MOE * TM_MOE


def _params(*sem):
    return pltpu.CompilerParams(dimension_semantics=sem, vmem_limit_bytes=VMEM_LIMIT)


def _mod_row(tile_idx, tm):
    tok0 = tile_idx * tm
    return jnp.where(tok0 < T_CTX, 0, 1 + (tok0 - T_CTX) // DEC_SEQ)


def _adaln_kernel(c_ref, w_ref, b_ref, o_ref):
    c = c_ref[...]
    a = (c * jax.nn.sigmoid(c)).astype(BF16)
    o_ref[...] = jnp.dot(a, w_ref[...].astype(BF16), preferred_element_type=F32) + b_ref[...]


def adaln_all(cond8, w_mod, b_mod):
    tn = 1024
    return pl.pallas_call(
        _adaln_kernel,
        grid=(DEPTH, N_MOD * D // tn),
        in_specs=[pl.BlockSpec((8, D), lambda l, j: (0, 0)),
                  pl.BlockSpec((None, D, tn), lambda l, j: (l, 0, j)),
                  pl.BlockSpec((None, 1, tn), lambda l, j: (l, 0, j))],
        out_specs=pl.BlockSpec((None, 8, tn), lambda l, j: (l, 0, j)),
        out_shape=jax.ShapeDtypeStruct((DEPTH, 8, N_MOD * D), F32),
        compiler_params=_params("arbitrary", "arbitrary"),
        name="adaln",
    )(cond8, w_mod, b_mod.reshape(DEPTH, 1, N_MOD * D))


def _modulated_norm(x, g, sh, sc):
    ms = jnp.mean(x * x, axis=-1, keepdims=True)
    return (x * lax.rsqrt(ms + EPS) * g) * (1.0 + sc) + sh


def _modulate_kernel(x_ref, g_ref, mod_ref, o_ref, *, k_shift):
    r = _mod_row(pl.program_id(0), TM_TOK)
    sh = mod_ref[pl.ds(r, 1), k_shift * D:(k_shift + 1) * D]
    sc = mod_ref[pl.ds(r, 1), (k_shift + 1) * D:(k_shift + 2) * D]
    o_ref[...] = _modulated_norm(x_ref[...], g_ref[...], sh, sc).astype(BF16)


def modulate(x, g, mod, k_shift):
    return pl.pallas_call(
        functools.partial(_modulate_kernel, k_shift=k_shift),
        grid=(T // TM_TOK,),
        in_specs=[pl.BlockSpec((TM_TOK, D), lambda i: (i, 0)),
                  pl.BlockSpec((1, D), lambda i: (0, 0)),
                  pl.BlockSpec((8, N_MOD * D), lambda i: (0, 0))],
        out_specs=pl.BlockSpec((TM_TOK, D), lambda i: (i, 0)),
        out_shape=jax.ShapeDtypeStruct((T, D), BF16),
        compiler_params=_params("arbitrary"),
        name="modulate",
    )(x, g.reshape(1, D), mod)


def _proj_kernel(x_ref, w_ref, o_ref, wbf_ref, *, act):
    @pl.when(pl.program_id(1) == 0)
    def _():
        wbf_ref[...] = w_ref[...].astype(BF16)

    y = jnp.dot(x_ref[...], wbf_ref[...], preferred_element_type=F32)
    if act == "sigmoid":
        y = jax.nn.sigmoid(y)
    o_ref[...] = y.astype(o_ref.dtype)


def proj(h, w, layer, col0, ncols, out_dtype, act, name):
    blk0 = col0 // TN_PROJ
    return pl.pallas_call(
        functools.partial(_proj_kernel, act=act),
        grid=(ncols // TN_PROJ, T // TM_PROJ),
        in_specs=[pl.BlockSpec((TM_PROJ, D), lambda j, i: (i, 0)),
                  pl.BlockSpec((None, D, TN_PROJ), lambda j, i: (layer, 0, blk0 + j))],
        out_specs=pl.BlockSpec((TM_PROJ, TN_PROJ), lambda j, i: (i, j)),
        out_shape=jax.ShapeDtypeStruct((T, ncols), out_dtype),
        scratch_shapes=[pltpu.VMEM((D, TN_PROJ), BF16)],
        compiler_params=_params("arbitrary", "arbitrary"),
        name=name,
    )(h, w)


def _norm_halves(x, g):
    lo = lax.broadcasted_iota(jnp.int32, x.shape, 1) < DF_QK
    x2 = x * x
    s_lo = jnp.sum(jnp.where(lo, x2, 0.0), axis=-1, keepdims=True)
    s_hi = jnp.sum(jnp.where(lo, 0.0, x2), axis=-1, keepdims=True)
    inv = jnp.where(lo, lax.rsqrt(s_lo * (1.0 / DF_QK) + EPS), lax.rsqrt(s_hi * (1.0 / DF_QK) + EPS))
    return x * inv * g


def _norm_full(x, g):
    ms = jnp.mean(x * x, axis=-1, keepdims=True)
    return x * lax.rsqrt(ms + EPS) * g


def _rope(x, cos, sin_a, sin_b, shift):
    return x * cos + pltpu.roll(x, LANE - shift, 1) * sin_a + pltpu.roll(x, shift, 1) * sin_b


def _softmax(s):
    m = jnp.max(s, axis=-1, keepdims=True)
    e = jnp.exp(s - m)
    return e / jnp.sum(e, axis=-1, keepdims=True)


def _dot_nt(a, b):
    return lax.dot_general(a, b, (((1,), (1,)), ((), ())), preferred_element_type=F32)


def _attn_kernel(*refs, lq, has_ctx, out_scale):
    if has_ctx:
        (dq_ref, dk_ref, dv_ref, gq_ref, gk_ref, gv_ref, cdk_ref, cdv_ref, cgk_ref, cgv_ref,
         cd_ref, sad_ref, sbd_ref, cg_ref, sag_ref, sbg_ref, cdq_ref, sadq_ref, sbdq_ref,
         cgq_ref, sagq_ref, sbgq_ref,
         qn_ref, kn_ref, gqn_ref, gkn_ref, sub_ref, lam_ref,
         yb_ref, yc_ref, kd_s, vd_s, kg_s, vg_s) = refs
    else:
        (dq_ref, dk_ref, dv_ref, gq_ref, gk_ref, gv_ref,
         qn_ref, kn_ref, gqn_ref, gkn_ref, sub_ref, lam_ref,
         yb_ref, yc_ref, ndk_ref, ndv_ref, ngk_ref, ngv_ref, kd_s, vd_s, kg_s, vg_s) = refs

    @pl.when(pl.program_id(1) == 0)
    def _():
        for h in range(DF_HEADS):
            cs = slice(h * LANE, (h + 1) * LANE)
            kn = _norm_halves(dk_ref[:, cs], kn_ref[...])
            if has_ctx:
                kn = _rope(kn, cd_ref[...], sad_ref[...], sbd_ref[...], DF_QK // 4)
            else:
                ndk_ref[:, cs] = kn
            kd_s[0:lq, cs] = kn.astype(BF16)
        vd_s[0:lq, :] = dv_ref[...].astype(BF16)
        for g in range(GQ_KV):
            cs = slice(g * LANE, (g + 1) * LANE)
            kn = _norm_full(gk_ref[:, cs], gkn_ref[...])
            if has_ctx:
                kn = _rope(kn, cg_ref[...], sag_ref[...], sbg_ref[...], GQ_DIM // 4)
            else:
                ngk_ref[:, cs] = kn
            kg_s[0:lq, cs] = kn.astype(BF16)
        vg_s[0:lq, :] = gv_ref[...].astype(BF16)
        if has_ctx:
            kd_s[lq:lq + PAST, :] = cdk_ref[...].astype(BF16)
            vd_s[lq:lq + PAST, :] = cdv_ref[...].astype(BF16)
            kg_s[lq:lq + PAST, :] = cgk_ref[...].astype(BF16)
            vg_s[lq:lq + PAST, :] = cgv_ref[...].astype(BF16)
        else:
            ndv_ref[...] = dv_ref[...]
            ngv_ref[...] = gv_ref[...]

    lam = lam_ref[:, 0:1]
    lo = lax.broadcasted_iota(jnp.int32, (QC, LANE), 1) < DF_QK

    for h in range(DF_HEADS):
        cs = slice(h * LANE, (h + 1) * LANE)
        qn = _norm_halves(dq_ref[:, cs], qn_ref[...])
        if has_ctx:
            qn = _rope(qn, cdq_ref[...], sadq_ref[...], sbdq_ref[...], DF_QK // 4)
        qn = qn * (DF_QK ** -0.5)
        kh = kd_s[:, cs]
        p0 = _softmax(_dot_nt(jnp.where(lo, qn, 0.0).astype(BF16), kh))
        p1 = _softmax(_dot_nt(jnp.where(lo, 0.0, qn).astype(BF16), kh))
        w = (p0 - lam * p1).astype(BF16)
        o = jnp.dot(w, vd_s[:, cs], preferred_element_type=F32)
        yb_ref[:, cs] = (_norm_full(o, sub_ref[...]) * out_scale).astype(yb_ref.dtype)

    for hq in range(GQ_HEADS):
        g = hq // (GQ_HEADS // GQ_KV)
        cs = slice(hq * LANE, (hq + 1) * LANE)
        ks = slice(g * LANE, (g + 1) * LANE)
        qn = _norm_full(gq_ref[:, cs], gqn_ref[...])
        if has_ctx:
            qn = _rope(qn, cgq_ref[...], sagq_ref[...], sbgq_ref[...], GQ_DIM // 4)
        s = _dot_nt(qn.astype(BF16), kg_s[:, ks]) * (GQ_DIM ** -0.5)
        p = _softmax(s).astype(BF16)
        yc_ref[:, cs] = jnp.dot(p, vg_s[:, ks], preferred_element_type=F32).astype(yc_ref.dtype)


def _rope_tables(d_head):
    t = jnp.arange(DEC_SEQ, dtype=jnp.int32)
    row, col = (t // GRID_W).astype(F32), (t % GRID_W).astype(F32)
    q = d_head // 4
    inv = ROPE_THETA ** (-jnp.arange(q, dtype=F32) / q)
    ar, ac = row[:, None] * inv, col[:, None] * inv
    z = jnp.zeros_like(ar)
    cos = jnp.concatenate([jnp.cos(ar), jnp.cos(ar), jnp.cos(ac), jnp.cos(ac)], -1)
    sin_a = jnp.concatenate([-jnp.sin(ar), z, -jnp.sin(ac), z], -1)
    sin_b = jnp.concatenate([z, jnp.sin(ar), z, jnp.sin(ac)], -1)
    rep = LANE // d_head
    return tuple(jnp.tile(a, (1, rep)) for a in (cos, sin_a, sin_b))


def attention(p1, gains, lam, layer_idx, caches=None):
    has_ctx = caches is not None
    lq = DEC_SEQ if has_ctx else SEQ
    n_seq = DEC_BATCH if has_ctx else BATCH
    row0 = (T_CTX // lq) if has_ctx else 0
    lk = lq + (PAST if has_ctx else 0)
    nq = lq // QC
    lam_init = 0.8 - 0.6 * math.exp(-0.3 * layer_idx)

    def qspec(width, col0):
        return pl.BlockSpec((QC, width), lambda b, q: ((row0 + b) * nq + q, col0 // width))

    def kspec(width, col0):
        return pl.BlockSpec((lq, width), lambda b, q: (row0 + b, col0 // width))

    const = lambda shape: pl.BlockSpec(shape, lambda b, q: (0,) * len(shape))
    in_specs = [qspec(512, C_DQ), kspec(512, C_DK), kspec(512, C_DV),
                qspec(1024, C_GQ), kspec(256, C_GK), kspec(256, C_GV)]
    args = [p1] * 6
    if has_ctx:
        cdk, cdv, cgk, cgv = caches
        in_specs += [pl.BlockSpec((None, PAST, 512), lambda b, q: (b, 0, 0)),
                     pl.BlockSpec((None, PAST, 512), lambda b, q: (b, 0, 0)),
                     pl.BlockSpec((None, PAST, 256), lambda b, q: (b, 0, 0)),
                     pl.BlockSpec((None, PAST, 256), lambda b, q: (b, 0, 0))]
        args += [cdk, cdv, cgk, cgv]
        td, tg = _rope_tables(DF_QK), _rope_tables(GQ_DIM)
        in_specs += [const((lq, LANE))] * 6 + [pl.BlockSpec((QC, LANE), lambda b, q: (q, 0))] * 6
        args += list(td) + list(tg) + list(td) + list(tg)
    in_specs += [const((1, LANE))] * 6
    args += list(gains) + [jnp.full((1, LANE), lam, F32)]

    out_specs = [pl.BlockSpec((QC, 512), lambda b, q: ((row0 + b) * nq + q - row0 * nq, 0)),
                 pl.BlockSpec((QC, 1024), lambda b, q: ((row0 + b) * nq + q - row0 * nq, 0))]
    n_rows = n_seq * lq
    out_shape = [jax.ShapeDtypeStruct((n_rows, 512), BF16), jax.ShapeDtypeStruct((n_rows, 1024), BF16)]
    if not has_ctx:
        out_specs += [pl.BlockSpec((lq, 512), lambda b, q: (b, 0)), pl.BlockSpec((lq, 512), lambda b, q: (b, 0)),
                      pl.BlockSpec((lq, 256), lambda b, q: (b, 0)), pl.BlockSpec((lq, 256), lambda b, q: (b, 0))]
        out_shape += [jax.ShapeDtypeStruct((n_rows, 512), F32), jax.ShapeDtypeStruct((n_rows, 512), F32),
                      jax.ShapeDtypeStruct((n_rows, 256), F32), jax.ShapeDtypeStruct((n_rows, 256), F32)]
    return pl.pallas_call(
        functools.partial(_attn_kernel, lq=lq, has_ctx=has_ctx, out_scale=1.0 - lam_init),
        grid=(n_seq, nq),
        in_specs=in_specs,
        out_specs=out_specs,
        out_shape=out_shape,
        scratch_shapes=[pltpu.VMEM((lk, 512), BF16), pltpu.VMEM((lk, 512), BF16),
                        pltpu.VMEM((lk, 256), BF16), pltpu.VMEM((lk, 256), BF16)],
        compiler_params=_params("arbitrary", "arbitrary"),
        name="attn_latent" if has_ctx else "attn_context",
    )(*args)


def _short_conv(u, w_ref, b_ref, length):
    row = lax.broadcasted_iota(jnp.int32, u.shape, 0)
    prev = jnp.where(row == 0, 0.0, pltpu.roll(u, 1, 0))
    nxt = jnp.where(row == length - 1, 0.0, pltpu.roll(u, length - 1, 0))
    return w_ref[0:1, :] * prev + w_ref[1:2, :] * u + w_ref[2:3, :] * nxt + b_ref[...]


def _hyena_kernel(v_ref, x_ref, cwv_ref, cbv_ref, cwx_ref, cbx_ref, skip_ref, fm_ref, gm_ref,
                  a_ref, b_ref, a2_ref, o_ref, z_s, *, length):
    n = pl.program_id(2)

    @pl.when(n == 0)
    def _():
        z_s[...] = _short_conv(v_ref[...], cwv_ref, cbv_ref, length)

    z = z_s[...]
    xn = _short_conv(x_ref[...], cwx_ref, cbx_ref, length)
    u = jnp.dot(fm_ref[...], z.astype(BF16), preferred_element_type=F32)
    ur, ui = u[:length], u[length:]
    a, b, a2 = a_ref[...], b_ref[...], a2_ref[...]
    yr = (ur * a - ui * b).astype(BF16)
    yi = (ur * b + ui * a2).astype(BF16)
    conv = (jnp.dot(gm_ref[:, :length], yr, preferred_element_type=F32)
            + jnp.dot(gm_ref[:, length:], yi, preferred_element_type=F32))
    z = xn * (conv + skip_ref[...] * z)
    z_s[...] = z
    o_ref[...] = z.astype(o_ref.dtype)


def _dft_matrices(length):
    n = 2 * length
    f = jnp.arange(length, dtype=jnp.int32)[:, None]
    t = jnp.arange(length, dtype=jnp.int32)[None, :]
    ang = ((f * t) % n).astype(F32) * (2.0 * math.pi / n)
    cos, sin = jnp.cos(ang), jnp.sin(ang)
    alt = jnp.where(t % 2 == 0, 1.0, -1.0).astype(F32)
    nsin = jnp.where(f == 0, jnp.broadcast_to(alt, sin.shape), -sin)
    fm = jnp.concatenate([cos, nsin], axis=0)
    wre = jnp.where(f == 0, 1.0 / n, 2.0 / n) * cos
    wim = jnp.where(f == 0, jnp.broadcast_to(alt, sin.shape) / n, -(2.0 / n) * sin)
    gm = jnp.concatenate([wre, wim], axis=0).T
    return fm.astype(BF16), gm.astype(BF16)


def _hyena_filter_spectra(length, w1, b1, w2, b2, w3, freq):
    hp = lax.Precision.HIGHEST
    t = jnp.linspace(0.0, 1.0, length, dtype=F32)[:, None]
    bands = jnp.linspace(1e-4, HY_BANDS - 1, HY_BANDS, dtype=F32)
    wpos = (2 * math.pi / length) * jnp.arange(length, dtype=F32)[:, None] * bands
    z = jnp.concatenate([t, jnp.cos(wpos), -jnp.sin(wpos)], -1)
    h = jnp.sin(freq * (jnp.dot(z, w1, precision=hp) + b1))
    h = jnp.sin(freq * (jnp.dot(h, w2, precision=hp) + b2))
    h = jnp.dot(h, w3, precision=hp).reshape(length, 2, 2, HY_CH)
    max_decay = math.log(1e-2) / 0.3
    min_decay = math.log(1e-2) / 1.5
    deltas = jnp.linspace(min_decay, max_decay, HY_CH, dtype=F32)
    h = h * jnp.exp(-t[:, :, None, None] * jnp.abs(deltas))
    fwd, bwd = h[:, :, 0], h[:, :, 1]
    k = jnp.concatenate([fwd, jnp.zeros((1, 2, HY_CH), F32), jnp.flip(bwd[1:], 0)], 0)
    kf = jnp.fft.rfft(k, axis=0)
    a = jnp.moveaxis(jnp.real(kf[:length]), 1, 0)
    b = jnp.moveaxis(jnp.imag(kf[:length]), 1, 0).at[:, 0].set(0.0)
    a2 = a.at[:, 0].set(jnp.real(kf[length]))
    return a, b, a2


def hyena(p1, conv_w, conv_b, skip, spectra, length, n_seq, row0, cw):
    fm, gm = _dft_matrices(length)
    a, b, a2 = spectra
    ncb = HY_CH // cw
    vblk = 2 * ncb
    const = lambda shape: pl.BlockSpec(shape, lambda s, c, n: (0,) * len(shape))
    spec_spec = pl.BlockSpec((None, length, cw), lambda s, c, n: (n, 0, c))
    return pl.pallas_call(
        functools.partial(_hyena_kernel, length=length),
        grid=(n_seq, ncb, 2),
        in_specs=[pl.BlockSpec((length, cw), lambda s, c, n: (row0 + s, vblk + c)),
                  pl.BlockSpec((length, cw), lambda s, c, n: (row0 + s, n * ncb + c)),
                  pl.BlockSpec((3, cw), lambda s, c, n: (0, vblk + c)),
                  pl.BlockSpec((1, cw), lambda s, c, n: (0, vblk + c)),
                  pl.BlockSpec((3, cw), lambda s, c, n: (0, n * ncb + c)),
                  pl.BlockSpec((1, cw), lambda s, c, n: (0, n * ncb + c)),
                  pl.BlockSpec((None, 1, cw), lambda s, c, n: (n, 0, c)),
                  const((2 * length, length)), const((length, 2 * length)),
                  spec_spec, spec_spec, spec_spec],
        out_specs=pl.BlockSpec((length, cw), lambda s, c, n: (s, c)),
        out_shape=jax.ShapeDtypeStruct((n_seq * length, HY_CH), BF16),
        scratch_shapes=[pltpu.VMEM((length, cw), F32)],
        compiler_params=_params("arbitrary", "arbitrary", "arbitrary"),
        name=f"hyena_{length}",
    )(p1, p1, conv_w, conv_b.reshape(1, -1), conv_w, conv_b.reshape(1, -1),
      skip.reshape(2, 1, HY_CH), fm, gm, a, b, a2)


def _merge_kernel(ya_ref, yb_ref, yc_ref, g_ref, wa_ref, wb_ref, wc_ref, o_ref):
    a = jnp.dot(ya_ref[...], wa_ref[...], preferred_element_type=F32)
    b = jnp.dot(yb_ref[...], wb_ref[...], preferred_element_type=F32)
    c = jnp.dot(yc_ref[...], wc_ref[...], preferred_element_type=F32)
    m = g_ref[:, 0:D] * a + g_ref[:, D:2 * D] * b + g_ref[:, 2 * D:3 * D] * c
    o_ref[...] = m.astype(o_ref.dtype)


def merge(ya, yb, yc, gates, wa, wb, wc):
    tm = 256
    const = lambda shape: pl.BlockSpec(shape, lambda i: (0, 0))
    return pl.pallas_call(
        _merge_kernel,
        grid=(T // tm,),
        in_specs=[pl.BlockSpec((tm, 512), lambda i: (i, 0)), pl.BlockSpec((tm, 512), lambda i: (i, 0)),
                  pl.BlockSpec((tm, 1024), lambda i: (i, 0)), pl.BlockSpec((tm, 3 * D), lambda i: (i, 0)),
                  const((512, D)), const((512, D)), const((1024, D))],
        out_specs=pl.BlockSpec((tm, D), lambda i: (i, 0)),
        out_shape=jax.ShapeDtypeStruct((T, D), BF16),
        compiler_params=_params("arbitrary"),
        name="merge",
    )(ya, yb, yc, gates, wa, wb, wc)


def _split3(x):
    hi = x.astype(BF16)
    return hi, (x - hi.astype(F32)).astype(BF16)


def _out_router_kernel(m_ref, x_ref, wout_ref, mod_ref, g_ref, rw_ref, rb_ref,
                       x1_ref, h2_ref, idx_ref, wt_ref, *, tm):
    r = _mod_row(pl.program_id(0), tm)
    o = jnp.dot(m_ref[...], wout_ref[...], preferred_element_type=F32)
    x1 = x_ref[...] + mod_ref[pl.ds(r, 1), 2 * D:3 * D] * o
    x1_ref[...] = x1
    h2 = _modulated_norm(x1, g_ref[...], mod_ref[pl.ds(r, 1), 3 * D:4 * D], mod_ref[pl.ds(r, 1), 4 * D:5 * D])
    h2_ref[...] = h2.astype(BF16)

    h_hi, h_lo = _split3(h2)
    w_hi, w_lo = _split3(rw_ref[...])
    logits = (jnp.dot(h_hi, w_hi, preferred_element_type=F32) + jnp.dot(h_hi, w_lo, preferred_element_type=F32)
              + jnp.dot(h_lo, w_hi, preferred_element_type=F32)) + rb_ref[...]

    lane = lax.broadcasted_iota(jnp.int32, logits.shape, 1)
    vals, idxs = [], []
    for _ in range(TOP_K):
        mx = jnp.max(logits, axis=-1, keepdims=True)
        ix = jnp.min(jnp.where(logits == mx, lane, LANE), axis=-1, keepdims=True)
        vals.append(mx)
        idxs.append(ix)
        logits = jnp.where(lane == ix, -jnp.inf, logits)
    es = [jnp.exp(v - vals[0]) for v in vals]
    inv = 1.0 / (es[0] + es[1] + es[2] + es[3])
    idx_out = jnp.zeros(lane.shape, jnp.int32)
    wt_out = jnp.zeros(lane.shape, F32)
    for k in range(TOP_K):
        idx_out = jnp.where(lane == k, idxs[k], idx_out)
        wt_out = jnp.where(lane == k, es[k] * inv, wt_out)
    idx_ref[...] = idx_out
    wt_ref[...] = wt_out


def out_router(merged, x, w_out, mod, norm2_g, router_w, router_b):
    tm = 256
    const = lambda shape: pl.BlockSpec(shape, lambda i: (0, 0))
    rw = jnp.zeros((D, LANE), F32).at[:, :N_EXPERTS].set(router_w)
    rb = jnp.full((1, LANE), -1e30, F32).at[0, :N_EXPERTS].set(router_b)
    return pl.pallas_call(
        functools.partial(_out_router_kernel, tm=tm),
        grid=(T // tm,),
        in_specs=[pl.BlockSpec((tm, D), lambda i: (i, 0)), pl.BlockSpec((tm, D), lambda i: (i, 0)),
                  const((D, D)), const((8, N_MOD * D)), const((1, D)), const((D, LANE)), const((1, LANE))],
        out_specs=[pl.BlockSpec((tm, D), lambda i: (i, 0)), pl.BlockSpec((tm, D), lambda i: (i, 0)),
                   pl.BlockSpec((tm, LANE), lambda i: (i, 0)), pl.BlockSpec((tm, LANE), lambda i: (i, 0))],
        out_shape=[jax.ShapeDtypeStruct((T, D), F32), jax.ShapeDtypeStruct((T, D), BF16),
                   jax.ShapeDtypeStruct((T, LANE), jnp.int32), jax.ShapeDtypeStruct((T, LANE), F32)],
        compiler_params=_params("arbitrary"),
        name="out_router",
    )(merged, x, w_out, mod, norm2_g.reshape(1, D), rw, rb)


def _route_plan(idx, wts, n_col_blocks):
    e = idx.reshape(-1)
    onehot = (e[:, None] == jnp.arange(N_EXPERTS, dtype=jnp.int32)[None, :]).astype(jnp.int32)
    csum = jnp.cumsum(onehot, axis=0)
    rank = jnp.sum(onehot * (csum - 1), axis=1)
    counts = csum[-1]
    padded = ((counts + TM_MOE - 1) // TM_MOE) * TM_MOE
    seg_end = jnp.cumsum(padded)
    seg_start = seg_end - padded
    pos = seg_start[e] + rank
    src = jnp.zeros((NPAD,), jnp.int32).at[pos].set(jnp.arange(N_ASSIGN, dtype=jnp.int32) // TOP_K)
    wrow = jnp.zeros((NPAD,), F32).at[pos].set(wts.reshape(-1))

    n_tiles = seg_end[-1] // TM_MOE
    step = jnp.arange(NT_MOE * n_col_blocks, dtype=jnp.int32)
    tile_probe = jnp.minimum(step // n_col_blocks, n_tiles - 1)
    ex = jnp.minimum(jnp.searchsorted(seg_end, tile_probe * TM_MOE, side="right"), N_EXPERTS - 1).astype(jnp.int32)
    t0 = seg_start[ex] // TM_MOE
    ne = jnp.maximum(padded[ex] // TM_MOE, 1)
    valid = step < n_tiles * n_col_blocks
    local = jnp.where(valid, step - n_col_blocks * t0, n_col_blocks * ne - 1)
    col = local // ne
    tile = t0 + local % ne
    first = jnp.logical_and(valid, local % ne == 0)
    return (pos, src, wrow, ex, col.astype(jnp.int32), tile.astype(jnp.int32),
            first.astype(jnp.int32), valid.astype(jnp.int32))


def _gate_up_kernel(ex_ref, col_ref, tile_ref, first_ref, valid_ref,
                    x_ref, wg_ref, wu_ref, bg_ref, bu_ref, o_ref, wg_s, wu_s):
    s = pl.program_id(0)

    @pl.when(first_ref[s] == 1)
    def _():
        wg_s[...] = wg_ref[...].astype(BF16)
        wu_s[...] = wu_ref[...].astype(BF16)

    @pl.when(valid_ref[s] == 1)
    def _():
        x = x_ref[...]
        g = jnp.dot(x, wg_s[...], preferred_element_type=F32) + bg_ref[...]
        u = jnp.dot(x, wu_s[...], preferred_element_type=F32) + bu_ref[...]
        gate = jnp.minimum(g, SWIGLU_LIMIT)
        up = jnp.clip(u, -SWIGLU_LIMIT, SWIGLU_LIMIT)
        o_ref[...] = (gate * jax.nn.sigmoid(SWIGLU_ALPHA * gate) * (up + 1.0)).astype(o_ref.dtype)


def gate_up(x_sorted, w_gu, b_gu, layer, plan):
    ex, col, tile, first, valid = plan
    ncb = D_FF // TN_GU
    grid_spec = pltpu.PrefetchScalarGridSpec(
        num_scalar_prefetch=5,
        grid=(NT_MOE * ncb,),
        in_specs=[pl.BlockSpec((TM_MOE, D), lambda s, ex, col, tile, first, valid: (tile[s], 0)),
                  pl.BlockSpec((None, None, D, TN_GU), lambda s, ex, col, tile, first, valid: (layer, ex[s], 0, col[s])),
                  pl.BlockSpec((None, None, D, TN_GU), lambda s, ex, col, tile, first, valid: (layer, ex[s], 0, ncb + col[s])),
                  pl.BlockSpec((None, None, 1, TN_GU), lambda s, ex, col, tile, first, valid: (layer, ex[s], 0, col[s])),
                  pl.BlockSpec((None, None, 1, TN_GU), lambda s, ex, col, tile, first, valid: (layer, ex[s], 0, ncb + col[s]))],
        out_specs=pl.BlockSpec((TM_MOE, TN_GU), lambda s, ex, col, tile, first, valid: (tile[s], col[s])),
        scratch_shapes=[pltpu.VMEM((D, TN_GU), BF16), pltpu.VMEM((D, TN_GU), BF16)],
    )
    b4 = b_gu.reshape(DEPTH, N_EXPERTS, 1, 2 * D_FF)
    return pl.pallas_call(
        _gate_up_kernel,
        grid_spec=grid_spec,
        out_shape=jax.ShapeDtypeStruct((NPAD, D_FF), BF16),
        compiler_params=_params("arbitrary"),
        name="moe_gate_up",
    )(ex, col, tile, first, valid, x_sorted, w_gu, w_gu, b4, b4)


def _down_kernel(ex_ref, col_ref, tile_ref, first_ref, valid_ref,
                 a_ref, w_ref, b_ref, wrow_ref, o_ref, w_s):
    s = pl.program_id(0)

    @pl.when(first_ref[s] == 1)
    def _():
        w_s[...] = w_ref[...].astype(BF16)

    @pl.when(valid_ref[s] == 1)
    def _():
        y = jnp.dot(a_ref[...], w_s[...], preferred_element_type=F32) + b_ref[...]
        o_ref[...] = y * wrow_ref[...]


def down(act, w_dn, b_dn, wrow, layer, plan):
    ex, col, tile, first, valid = plan
    ncb = D // TN_DN
    grid_spec = pltpu.PrefetchScalarGridSpec(
        num_scalar_prefetch=5,
        grid=(NT_MOE * ncb,),
        in_specs=[pl.BlockSpec((TM_MOE, D_FF), lambda s, ex, col, tile, first, valid: (tile[s], 0)),
                  pl.BlockSpec((None, None, D_FF, TN_DN), lambda s, ex, col, tile, first, valid: (layer, ex[s], 0, col[s])),
                  pl.BlockSpec((None, None, 1, TN_DN), lambda s, ex, col, tile, first, valid: (layer, ex[s], 0, col[s])),
                  pl.BlockSpec((TM_MOE, 1), lambda s, ex, col, tile, first, valid: (tile[s], 0))],
        out_specs=pl.BlockSpec((TM_MOE, TN_DN), lambda s, ex, col, tile, first, valid: (tile[s], col[s])),
        scratch_shapes=[pltpu.VMEM((D_FF, TN_DN), BF16)],
    )
    return pl.pallas_call(
        _down_kernel,
        grid_spec=grid_spec,
        out_shape=jax.ShapeDtypeStruct((NPAD, D), F32),
        compiler_params=_params("arbitrary"),
        name="moe_down",
    )(ex, col, tile, first, valid, act, w_dn, b_dn.reshape(DEPTH, N_EXPERTS, 1, D), wrow.reshape(NPAD, 1))


def _residual_kernel(x_ref, m_ref, mod_ref, o_ref):
    r = _mod_row(pl.program_id(0), TM_TOK)
    o_ref[...] = x_ref[...] + mod_ref[pl.ds(r, 1), 5 * D:6 * D] * m_ref[...]


def residual(x, moe_out, mod):
    return pl.pallas_call(
        _residual_kernel,
        grid=(T // TM_TOK,),
        in_specs=[pl.BlockSpec((TM_TOK, D), lambda i: (i, 0)), pl.BlockSpec((TM_TOK, D), lambda i: (i, 0)),
                  pl.BlockSpec((8, N_MOD * D), lambda i: (0, 0))],
        out_specs=pl.BlockSpec((TM_TOK, D), lambda i: (i, 0)),
        out_shape=jax.ShapeDtypeStruct((T, D), F32),
        compiler_params=_params("arbitrary"),
        name="residual",
    )(x, moe_out, mod)


def kernel(x_prompt, x_sample, cache_diff_k, cache_diff_v, cache_gqa_k, cache_gqa_v, c, c_ctx, w_mod, b_mod, norm1_g, norm2_g, w_in, hy_conv_w, hy_conv_b, hy_f_w1, hy_f_b1, hy_f_w2, hy_f_b2, hy_f_w3, hy_freq, hy_skip, df_qn_g, df_kn_g, df_lq1, df_lk1, df_lq2, df_lk2, df_subln_g, gq_qn_g, gq_kn_g, w_hy_o, w_df_o, w_gq_o, w_out, router_w, router_b, w_gu, b_gu, w_dn, b_dn):
    x = jnp.concatenate([x_prompt.reshape(T_CTX, D), x_sample.reshape(T_LAT, D)], axis=0)
    cond8 = jnp.zeros((8, D), F32).at[0].set(c_ctx).at[1:1 + DEC_BATCH].set(c)
    mods = adaln_all(cond8, w_mod, b_mod)

    cdk = cache_diff_k.reshape(DEC_BATCH, DEPTH, PAST, 512)
    cdv = cache_diff_v.reshape(DEC_BATCH, DEPTH, PAST, 512)
    cgk = cache_gqa_k.reshape(DEC_BATCH, DEPTH, PAST, 256)
    cgv = cache_gqa_v.reshape(DEC_BATCH, DEPTH, PAST, 256)

    new_dk, new_dv, new_gk, new_gv = [], [], [], []
    for l in range(DEPTH):
        mod = mods[l]
        h = modulate(x, norm1_g[l], mod, 0)
        p1 = proj(h, w_in, l, 0, C_GATE, F32, None, "proj_mix")
        gates = proj(h, w_in, l, C_GATE, 3 * D, BF16, "sigmoid", "proj_gates")

        tile128 = lambda g: jnp.tile(g, LANE // g.shape[0]).reshape(1, LANE)
        gains = (tile128(df_qn_g[l]), tile128(df_kn_g[l]), tile128(gq_qn_g[l]), tile128(gq_kn_g[l]),
                 tile128(df_subln_g[l]))
        lam_init = 0.8 - 0.6 * math.exp(-0.3 * l)
        lam = (jnp.exp(jnp.sum(df_lq1[l] * df_lk1[l])) - jnp.exp(jnp.sum(df_lq2[l] * df_lk2[l])) + lam_init)

        yb_c, yc_c, ndk, ndv, ngk, ngv = attention(p1, gains, lam, l)
        yb_l, yc_l = attention(p1, gains, lam, l, caches=(cdk[:, l], cdv[:, l], cgk[:, l], cgv[:, l]))
        new_dk.append(ndk.reshape(BATCH, SEQ, DF_HEADS, 2, DF_QK))
        new_dv.append(ndv.reshape(BATCH, SEQ, DF_HEADS, DF_V))
        new_gk.append(ngk.reshape(BATCH, SEQ, GQ_KV, GQ_DIM))
        new_gv.append(ngv.reshape(BATCH, SEQ, GQ_KV, GQ_DIM))

        filt = (hy_f_w1[l], hy_f_b1[l], hy_f_w2[l], hy_f_b2[l], hy_f_w3[l], hy_freq[l])
        ya_c = hyena(p1, hy_conv_w[l], hy_conv_b[l], hy_skip[l], _hyena_filter_spectra(SEQ, *filt),
                     SEQ, BATCH, 0, 512)
        ya_l = hyena(p1, hy_conv_w[l], hy_conv_b[l], hy_skip[l], _hyena_filter_spectra(DEC_SEQ, *filt),
                     DEC_SEQ, DEC_BATCH, T_CTX // DEC_SEQ, 256)

        ya = jnp.concatenate([ya_c, ya_l], axis=0)
        yb = jnp.concatenate([yb_c, yb_l], axis=0)
        yc = jnp.concatenate([yc_c, yc_l], axis=0)
        merged = merge(ya, yb, yc, gates, w_hy_o[l].astype(BF16), w_df_o[l].astype(BF16), w_gq_o[l].astype(BF16))
        x1, h2, idx, wts = out_router(merged, x, w_out[l].astype(BF16), mod, norm2_g[l], router_w[l], router_b[l])

        idx4, wts4 = idx[:, :TOP_K], wts[:, :TOP_K]
        pos, src, wrow, *plan = _route_plan(idx4, wts4, D_FF // TN_GU)
        x_sorted = jnp.take(h2, src, axis=0)
        act = gate_up(x_sorted, w_gu, b_gu, l, plan)
        ys = down(act, w_dn, b_dn, wrow, l, plan)
        moe_out = jnp.sum(jnp.take(ys, pos, axis=0).reshape(T, TOP_K, D), axis=1)
        x = residual(x1, moe_out, mod)

    y_prompt = x[:T_CTX].reshape(BATCH, SEQ, D)
    y_sample = x[T_CTX:].reshape(DEC_BATCH, DEC_SEQ, D)
    return (y_prompt, y_sample, jnp.stack(new_dk, axis=1), jnp.stack(new_dv, axis=1),
            jnp.stack(new_gk, axis=1), jnp.stack(new_gv, axis=1))
```

```python
import functools
import math

import jax
import jax.numpy as jnp
import numpy as np
from jax import lax
from jax.experimental import pallas as pl
from jax.experimental.pallas import tpu as pltpu

F32 = jnp.float32
BF16 = jnp.bfloat16

D = 2048
BATCH, SEQ = 32, 256
DEPTH = 2
DEC_BATCH, DEC_SEQ = 2, 1024
PAST = 256
GRID_W = 64
ROPE_THETA = 10000.0
EPS = 1e-6
N_MOD = 6
HY_CH = 512
HY_BANDS = 16
HY_FFN = 64
DF_HEADS, DF_QK, DF_V = 4, 64, 128
GQ_HEADS, GQ_KV, GQ_DIM = 8, 2, 128
N_EXPERTS, TOP_K = 32, 4
D_FF = D
SWIGLU_LIMIT = 7.0
SWIGLU_ALPHA = 1.702

T_CTX = BATCH * SEQ
T_LAT = DEC_BATCH * DEC_SEQ
T = T_CTX + T_LAT
N_ASSIGN = T * TOP_K

C_HY = 0
C_DQ = 3 * HY_CH
C_DK = C_DQ + 512
C_DV = C_DK + 512
C_GQ = C_DV + 512
C_GK = C_GQ + 1024
C_GV = C_GK + 256
C_GATE = C_GV + 256
D_IN = C_GATE + 3 * D

LANE = 128
VMEM_LIMIT = 56 * 1024 * 1024

TM_TOK = 512
TM_PROJ = 1024
TN_PROJ = 512
QC = 256
TM_MOE = 512
TN_GU = 512
TN_DN = 1024
SLAB = 16
YSLAB = TN_DN // LANE
TM_ROUTE = 256
NT_MOE = N_ASSIGN // TM_MOE + N_EXPERTS
NPAD = NT_MOE * TM_MOE


def _params(*sem):
    return pltpu.CompilerParams(dimension_semantics=sem, vmem_limit_bytes=VMEM_LIMIT)


def _mod_row(tile_idx, tm):
    tok0 = tile_idx * tm
    return jnp.where(tok0 < T_CTX, 0, 1 + (tok0 - T_CTX) // DEC_SEQ)


def _adaln_kernel(c_ref, w_ref, b_ref, o_ref):
    c = c_ref[...]
    a = (c * jax.nn.sigmoid(c)).astype(BF16)
    o_ref[...] = jnp.dot(a, w_ref[...].astype(BF16), preferred_element_type=F32) + b_ref[...]


def adaln_all(cond8, w_mod, b_mod):
    tn = 1024
    return pl.pallas_call(
        _adaln_kernel,
        grid=(DEPTH, N_MOD * D // tn),
        in_specs=[pl.BlockSpec((8, D), lambda l, j: (0, 0)),
                  pl.BlockSpec((None, D, tn), lambda l, j: (l, 0, j)),
                  pl.BlockSpec((None, 1, tn), lambda l, j: (l, 0, j))],
        out_specs=pl.BlockSpec((None, 8, tn), lambda l, j: (l, 0, j)),
        out_shape=jax.ShapeDtypeStruct((DEPTH, 8, N_MOD * D), F32),
        compiler_params=_params("arbitrary", "arbitrary"),
        name="adaln",
    )(cond8, w_mod, b_mod.reshape(DEPTH, 1, N_MOD * D))


def _modulated_norm(x, g, sh, sc):
    ms = jnp.mean(x * x, axis=-1, keepdims=True)
    return (x * lax.rsqrt(ms + EPS) * g) * (1.0 + sc) + sh


def _modulate_kernel(x_ref, g_ref, mod_ref, o_ref, *, k_shift):
    r = _mod_row(pl.program_id(0), TM_TOK)
    sh = mod_ref[pl.ds(r, 1), k_shift * D:(k_shift + 1) * D]
    sc = mod_ref[pl.ds(r, 1), (k_shift + 1) * D:(k_shift + 2) * D]
    o_ref[...] = _modulated_norm(x_ref[...], g_ref[...], sh, sc).astype(BF16)


def modulate(x, g, mod, k_shift):
    return pl.pallas_call(
        functools.partial(_modulate_kernel, k_shift=k_shift),
        grid=(T // TM_TOK,),
        in_specs=[pl.BlockSpec((TM_TOK, D), lambda i: (i, 0)),
                  pl.BlockSpec((1, D), lambda i: (0, 0)),
                  pl.BlockSpec((8, N_MOD * D), lambda i: (0, 0))],
        out_specs=pl.BlockSpec((TM_TOK, D), lambda i: (i, 0)),
        out_shape=jax.ShapeDtypeStruct((T, D), BF16),
        compiler_params=_params("arbitrary"),
        name="modulate",
    )(x, g.reshape(1, D), mod)


def _proj_kernel(x_ref, w_ref, o_ref, wbf_ref, *, act):
    @pl.when(pl.program_id(1) == 0)
    def _():
        wbf_ref[...] = w_ref[...].astype(BF16)

    y = jnp.dot(x_ref[...], wbf_ref[...], preferred_element_type=F32)
    if act == "sigmoid":
        y = jax.nn.sigmoid(y)
    o_ref[...] = y.astype(o_ref.dtype)


def proj(h, w, layer, col0, ncols, out_dtype, act, name):
    blk0 = col0 // TN_PROJ
    return pl.pallas_call(
        functools.partial(_proj_kernel, act=act),
        grid=(ncols // TN_PROJ, T // TM_PROJ),
        in_specs=[pl.BlockSpec((TM_PROJ, D), lambda j, i: (i, 0)),
                  pl.BlockSpec((None, D, TN_PROJ), lambda j, i: (layer, 0, blk0 + j))],
        out_specs=pl.BlockSpec((TM_PROJ, TN_PROJ), lambda j, i: (i, j)),
        out_shape=jax.ShapeDtypeStruct((T, ncols), out_dtype),
        scratch_shapes=[pltpu.VMEM((D, TN_PROJ), BF16)],
        compiler_params=_params("arbitrary", "arbitrary"),
        name=name,
    )(h, w)


def _norm_halves(x, g):
    lo = lax.broadcasted_iota(jnp.int32, x.shape, 1) < DF_QK
    x2 = x * x
    s_lo = jnp.sum(jnp.where(lo, x2, 0.0), axis=-1, keepdims=True)
    s_hi = jnp.sum(jnp.where(lo, 0.0, x2), axis=-1, keepdims=True)
    inv = jnp.where(lo, lax.rsqrt(s_lo * (1.0 / DF_QK) + EPS), lax.rsqrt(s_hi * (1.0 / DF_QK) + EPS))
    return x * inv * g


def _norm_full(x, g):
    ms = jnp.mean(x * x, axis=-1, keepdims=True)
    return x * lax.rsqrt(ms + EPS) * g


def _rope(x, cos, sin_a, sin_b, shift):
    return x * cos + pltpu.roll(x, LANE - shift, 1) * sin_a + pltpu.roll(x, shift, 1) * sin_b


def _softmax(s):
    m = jnp.max(s, axis=-1, keepdims=True)
    e = jnp.exp(s - m)
    return e / jnp.sum(e, axis=-1, keepdims=True)


def _dot_nt(a, b):
    return lax.dot_general(a, b, (((1,), (1,)), ((), ())), preferred_element_type=F32)


def _attn_kernel(*refs, lq, has_ctx, out_scale):
    if has_ctx:
        (dq_ref, dk_ref, dv_ref, gq_ref, gk_ref, gv_ref, cdk_ref, cdv_ref, cgk_ref, cgv_ref,
         cd_ref, sad_ref, sbd_ref, cg_ref, sag_ref, sbg_ref, cdq_ref, sadq_ref, sbdq_ref,
         cgq_ref, sagq_ref, sbgq_ref,
         qn_ref, kn_ref, gqn_ref, gkn_ref, sub_ref, lam_ref,
         yb_ref, yc_ref, kd_s, vd_s, kg_s, vg_s) = refs
    else:
        (dq_ref, dk_ref, dv_ref, gq_ref, gk_ref, gv_ref,
         qn_ref, kn_ref, gqn_ref, gkn_ref, sub_ref, lam_ref,
         yb_ref, yc_ref, ndk_ref, ndv_ref, ngk_ref, ngv_ref, kd_s, vd_s, kg_s, vg_s) = refs

    @pl.when(pl.program_id(1) == 0)
    def _():
        for h in range(DF_HEADS):
            cs = slice(h * LANE, (h + 1) * LANE)
            kn = _norm_halves(dk_ref[:, cs], kn_ref[...])
            if has_ctx:
                kn = _rope(kn, cd_ref[...], sad_ref[...], sbd_ref[...], DF_QK // 4)
            else:
                ndk_ref[:, cs] = kn
            kd_s[0:lq, cs] = kn.astype(BF16)
        vd_s[0:lq, :] = dv_ref[...].astype(BF16)
        for g in range(GQ_KV):
            cs = slice(g * LANE, (g + 1) * LANE)
            kn = _norm_full(gk_ref[:, cs], gkn_ref[...])
            if has_ctx:
                kn = _rope(kn, cg_ref[...], sag_ref[...], sbg_ref[...], GQ_DIM // 4)
            else:
                ngk_ref[:, cs] = kn
            kg_s[0:lq, cs] = kn.astype(BF16)
        vg_s[0:lq, :] = gv_ref[...].astype(BF16)
        if has_ctx:
            kd_s[lq:lq + PAST, :] = cdk_ref[...].astype(BF16)
            vd_s[lq:lq + PAST, :] = cdv_ref[...].astype(BF16)
            kg_s[lq:lq + PAST, :] = cgk_ref[...].astype(BF16)
            vg_s[lq:lq + PAST, :] = cgv_ref[...].astype(BF16)
        else:
            ndv_ref[...] = dv_ref[...]
            ngv_ref[...] = gv_ref[...]

    lam = lam_ref[:, 0:1]
    lo = lax.broadcasted_iota(jnp.int32, (QC, LANE), 1) < DF_QK

    for h in range(DF_HEADS):
        cs = slice(h * LANE, (h + 1) * LANE)
        qn = _norm_halves(dq_ref[:, cs], qn_ref[...])
        if has_ctx:
            qn = _rope(qn, cdq_ref[...], sadq_ref[...], sbdq_ref[...], DF_QK // 4)
        qn = qn * (DF_QK ** -0.5)
        kh = kd_s[:, cs]
        p0 = _softmax(_dot_nt(jnp.where(lo, qn, 0.0).astype(BF16), kh))
        p1 = _softmax(_dot_nt(jnp.where(lo, 0.0, qn).astype(BF16), kh))
        w = (p0 - lam * p1).astype(BF16)
        o = jnp.dot(w, vd_s[:, cs], preferred_element_type=F32)
        yb_ref[:, cs] = (_norm_full(o, sub_ref[...]) * out_scale).astype(yb_ref.dtype)

    for hq in range(GQ_HEADS):
        g = hq // (GQ_HEADS // GQ_KV)
        cs = slice(hq * LANE, (hq + 1) * LANE)
        ks = slice(g * LANE, (g + 1) * LANE)
        qn = _norm_full(gq_ref[:, cs], gqn_ref[...])
        if has_ctx:
            qn = _rope(qn, cgq_ref[...], sagq_ref[...], sbgq_ref[...], GQ_DIM // 4)
        s = _dot_nt(qn.astype(BF16), kg_s[:, ks]) * (GQ_DIM ** -0.5)
        p = _softmax(s).astype(BF16)
        yc_ref[:, cs] = jnp.dot(p, vg_s[:, ks], preferred_element_type=F32).astype(yc_ref.dtype)


def _rope_tables(d_head):
    t = np.arange(DEC_SEQ)
    row, col = (t // GRID_W).astype(np.float64), (t % GRID_W).astype(np.float64)
    q = d_head // 4
    inv = ROPE_THETA ** (-np.arange(q, dtype=np.float64) / q)
    ar, ac = row[:, None] * inv, col[:, None] * inv
    z = np.zeros_like(ar)
    cos = np.concatenate([np.cos(ar), np.cos(ar), np.cos(ac), np.cos(ac)], -1)
    sin_a = np.concatenate([-np.sin(ar), z, -np.sin(ac), z], -1)
    sin_b = np.concatenate([z, np.sin(ar), z, np.sin(ac)], -1)
    rep = LANE // d_head
    return tuple(jnp.asarray(np.tile(a, (1, rep)), F32) for a in (cos, sin_a, sin_b))


def attention(p1, gains, lam, layer_idx, caches=None):
    has_ctx = caches is not None
    lq = DEC_SEQ if has_ctx else SEQ
    n_seq = DEC_BATCH if has_ctx else BATCH
    row0 = (T_CTX // lq) if has_ctx else 0
    lk = lq + (PAST if has_ctx else 0)
    nq = lq // QC
    lam_init = 0.8 - 0.6 * math.exp(-0.3 * layer_idx)

    def qspec(width, col0):
        return pl.BlockSpec((QC, width), lambda b, q: ((row0 + b) * nq + q, col0 // width))

    def kspec(width, col0):
        return pl.BlockSpec((lq, width), lambda b, q: (row0 + b, col0 // width))

    const = lambda shape: pl.BlockSpec(shape, lambda b, q: (0,) * len(shape))
    in_specs = [qspec(512, C_DQ), kspec(512, C_DK), kspec(512, C_DV),
                qspec(1024, C_GQ), kspec(256, C_GK), kspec(256, C_GV)]
    args = [p1] * 6
    if has_ctx:
        cdk, cdv, cgk, cgv = caches
        in_specs += [pl.BlockSpec((None, PAST, 512), lambda b, q: (b, 0, 0)),
                     pl.BlockSpec((None, PAST, 512), lambda b, q: (b, 0, 0)),
                     pl.BlockSpec((None, PAST, 256), lambda b, q: (b, 0, 0)),
                     pl.BlockSpec((None, PAST, 256), lambda b, q: (b, 0, 0))]
        args += [cdk, cdv, cgk, cgv]
        td, tg = _rope_tables(DF_QK), _rope_tables(GQ_DIM)
        in_specs += [const((lq, LANE))] * 6 + [pl.BlockSpec((QC, LANE), lambda b, q: (q, 0))] * 6
        args += list(td) + list(tg) + list(td) + list(tg)
    in_specs += [const((1, LANE))] * 6
    args += list(gains) + [jnp.full((1, LANE), lam, F32)]

    out_specs = [pl.BlockSpec((QC, 512), lambda b, q: ((row0 + b) * nq + q - row0 * nq, 0)),
                 pl.BlockSpec((QC, 1024), lambda b, q: ((row0 + b) * nq + q - row0 * nq, 0))]
    n_rows = n_seq * lq
    out_shape = [jax.ShapeDtypeStruct((n_rows, 512), BF16), jax.ShapeDtypeStruct((n_rows, 1024), BF16)]
    if not has_ctx:
        out_specs += [pl.BlockSpec((lq, 512), lambda b, q: (b, 0)), pl.BlockSpec((lq, 512), lambda b, q: (b, 0)),
                      pl.BlockSpec((lq, 256), lambda b, q: (b, 0)), pl.BlockSpec((lq, 256), lambda b, q: (b, 0))]
        out_shape += [jax.ShapeDtypeStruct((n_rows, 512), F32), jax.ShapeDtypeStruct((n_rows, 512), F32),
                      jax.ShapeDtypeStruct((n_rows, 256), F32), jax.ShapeDtypeStruct((n_rows, 256), F32)]
    return pl.pallas_call(
        functools.partial(_attn_kernel, lq=lq, has_ctx=has_ctx, out_scale=1.0 - lam_init),
        grid=(n_seq, nq),
        in_specs=in_specs,
        out_specs=out_specs,
        out_shape=out_shape,
        scratch_shapes=[pltpu.VMEM((lk, 512), BF16), pltpu.VMEM((lk, 512), BF16),
                        pltpu.VMEM((lk, 256), BF16), pltpu.VMEM((lk, 256), BF16)],
        compiler_params=_params("arbitrary", "arbitrary"),
        name="attn_latent" if has_ctx else "attn_context",
    )(*args)


def _short_conv(u, w_ref, b_ref, length):
    row = lax.broadcasted_iota(jnp.int32, u.shape, 0)
    prev = jnp.where(row == 0, 0.0, pltpu.roll(u, 1, 0))
    nxt = jnp.where(row == length - 1, 0.0, pltpu.roll(u, length - 1, 0))
    return w_ref[0:1, :] * prev + w_ref[1:2, :] * u + w_ref[2:3, :] * nxt + b_ref[...]


def _hyena_kernel(v_ref, x_ref, cwv_ref, cbv_ref, cwx_ref, cbx_ref, skip_ref, fm_ref, gm_ref,
                  a_ref, b_ref, a2_ref, o_ref, z_s, *, length):
    n = pl.program_id(2)

    @pl.when(n == 0)
    def _():
        z_s[...] = _short_conv(v_ref[...], cwv_ref, cbv_ref, length)

    z = z_s[...]
    xn = _short_conv(x_ref[...], cwx_ref, cbx_ref, length)
    u = jnp.dot(fm_ref[...], z.astype(BF16), preferred_element_type=F32)
    ur, ui = u[:length], u[length:]
    a, b, a2 = a_ref[...], b_ref[...], a2_ref[...]
    yr = (ur * a - ui * b).astype(BF16)
    yi = (ur * b + ui * a2).astype(BF16)
    conv = (jnp.dot(gm_ref[:, :length], yr, preferred_element_type=F32)
            + jnp.dot(gm_ref[:, length:], yi, preferred_element_type=F32))
    z = xn * (conv + skip_ref[...] * z)
    z_s[...] = z
    o_ref[...] = z.astype(o_ref.dtype)


def _dft_matrices(length):
    n = 2 * length
    f = np.arange(length)[:, None]
    t = np.arange(length)[None, :]
    ang = ((f * t) % n).astype(np.float64) * (2.0 * math.pi / n)
    cos, sin = np.cos(ang), np.sin(ang)
    alt = np.broadcast_to(np.where(t % 2 == 0, 1.0, -1.0), sin.shape)
    fm = np.concatenate([cos, np.where(f == 0, alt, -sin)], axis=0)
    wre = np.where(f == 0, 1.0 / n, 2.0 / n) * cos
    wim = np.where(f == 0, alt / n, -(2.0 / n) * sin)
    gm = np.concatenate([wre, wim], axis=0).T
    return jnp.asarray(fm, BF16), jnp.asarray(gm, BF16)


def _hyena_filter_spectra(length, w1, b1, w2, b2, w3, freq):
    hp = lax.Precision.HIGHEST
    t = jnp.linspace(0.0, 1.0, length, dtype=F32)[:, None]
    bands = jnp.linspace(1e-4, HY_BANDS - 1, HY_BANDS, dtype=F32)
    wpos = (2 * math.pi / length) * jnp.arange(length, dtype=F32)[:, None] * bands
    z = jnp.concatenate([t, jnp.cos(wpos), -jnp.sin(wpos)], -1)
    h = jnp.sin(freq * (jnp.dot(z, w1, precision=hp) + b1))
    h = jnp.sin(freq * (jnp.dot(h, w2, precision=hp) + b2))
    h = jnp.dot(h, w3, precision=hp).reshape(length, 2, 2, HY_CH)
    max_decay = math.log(1e-2) / 0.3
    min_decay = math.log(1e-2) / 1.5
    deltas = jnp.linspace(min_decay, max_decay, HY_CH, dtype=F32)
    h = h * jnp.exp(-t[:, :, None, None] * jnp.abs(deltas))
    fwd, bwd = h[:, :, 0], h[:, :, 1]
    k = jnp.concatenate([fwd, jnp.zeros((1, 2, HY_CH), F32), jnp.flip(bwd[1:], 0)], 0)
    kf = jnp.fft.rfft(k, axis=0)
    a = jnp.moveaxis(jnp.real(kf[:length]), 1, 0)
    b = jnp.moveaxis(jnp.imag(kf[:length]), 1, 0).at[:, 0].set(0.0)
    a2 = a.at[:, 0].set(jnp.real(kf[length]))
    return a, b, a2


def hyena(p1, conv_w, conv_b, skip, spectra, length, n_seq, row0, cw):
    fm, gm = _dft_matrices(length)
    a, b, a2 = spectra
    ncb = HY_CH // cw
    vblk = 2 * ncb
    const = lambda shape: pl.BlockSpec(shape, lambda s, c, n: (0,) * len(shape))
    spec_spec = pl.BlockSpec((None, length, cw), lambda s, c, n: (n, 0, c))
    return pl.pallas_call(
        functools.partial(_hyena_kernel, length=length),
        grid=(n_seq, ncb, 2),
        in_specs=[pl.BlockSpec((length, cw), lambda s, c, n: (row0 + s, vblk + c)),
                  pl.BlockSpec((length, cw), lambda s, c, n: (row0 + s, n * ncb + c)),
                  pl.BlockSpec((3, cw), lambda s, c, n: (0, vblk + c)),
                  pl.BlockSpec((1, cw), lambda s, c, n: (0, vblk + c)),
                  pl.BlockSpec((3, cw), lambda s, c, n: (0, n * ncb + c)),
                  pl.BlockSpec((1, cw), lambda s, c, n: (0, n * ncb + c)),
                  pl.BlockSpec((None, 1, cw), lambda s, c, n: (n, 0, c)),
                  const((2 * length, length)), const((length, 2 * length)),
                  spec_spec, spec_spec, spec_spec],
        out_specs=pl.BlockSpec((length, cw), lambda s, c, n: (s, c)),
        out_shape=jax.ShapeDtypeStruct((n_seq * length, HY_CH), BF16),
        scratch_shapes=[pltpu.VMEM((length, cw), F32)],
        compiler_params=_params("arbitrary", "arbitrary", "arbitrary"),
        name=f"hyena_{length}",
    )(p1, p1, conv_w, conv_b.reshape(1, -1), conv_w, conv_b.reshape(1, -1),
      skip.reshape(2, 1, HY_CH), fm, gm, a, b, a2)


def _merge_kernel(ya_ref, yb_ref, yc_ref, g_ref, wa_ref, wb_ref, wc_ref, o_ref):
    a = jnp.dot(ya_ref[...], wa_ref[...], preferred_element_type=F32)
    b = jnp.dot(yb_ref[...], wb_ref[...], preferred_element_type=F32)
    c = jnp.dot(yc_ref[...], wc_ref[...], preferred_element_type=F32)
    m = g_ref[:, 0:D] * a + g_ref[:, D:2 * D] * b + g_ref[:, 2 * D:3 * D] * c
    o_ref[...] = m.astype(o_ref.dtype)


def merge(ya, yb, yc, gates, wa, wb, wc):
    tm = 256
    const = lambda shape: pl.BlockSpec(shape, lambda i: (0, 0))
    return pl.pallas_call(
        _merge_kernel,
        grid=(T // tm,),
        in_specs=[pl.BlockSpec((tm, 512), lambda i: (i, 0)), pl.BlockSpec((tm, 512), lambda i: (i, 0)),
                  pl.BlockSpec((tm, 1024), lambda i: (i, 0)), pl.BlockSpec((tm, 3 * D), lambda i: (i, 0)),
                  const((512, D)), const((512, D)), const((1024, D))],
        out_specs=pl.BlockSpec((tm, D), lambda i: (i, 0)),
        out_shape=jax.ShapeDtypeStruct((T, D), BF16),
        compiler_params=_params("arbitrary"),
        name="merge",
    )(ya, yb, yc, gates, wa, wb, wc)


def _split3(x):
    hi = x.astype(BF16)
    return hi, (x - hi.astype(F32)).astype(BF16)


def _out_router_kernel(m_ref, x_ref, wout_ref, mod_ref, g_ref, rw_ref, rb_ref,
                       x1_ref, h2_ref, route_ref, wt_ref, cnt_ref, cnt_s, *, tm):
    r = _mod_row(pl.program_id(0), tm)
    o = jnp.dot(m_ref[...], wout_ref[...], preferred_element_type=F32)
    x1 = x_ref[...] + mod_ref[pl.ds(r, 1), 2 * D:3 * D] * o
    x1_ref[...] = x1
    h2 = _modulated_norm(x1, g_ref[...], mod_ref[pl.ds(r, 1), 3 * D:4 * D], mod_ref[pl.ds(r, 1), 4 * D:5 * D])
    for s in range(SLAB):
        h2_ref[pl.ds(s, tm, stride=SLAB), :] = h2[:, s * LANE:(s + 1) * LANE]

    h_hi, h_lo = _split3(h2)
    w_hi, w_lo = _split3(rw_ref[...])
    logits = (jnp.dot(h_hi, w_hi, preferred_element_type=F32) + jnp.dot(h_hi, w_lo, preferred_element_type=F32)
              + jnp.dot(h_lo, w_hi, preferred_element_type=F32)) + rb_ref[...]

    lane = lax.broadcasted_iota(jnp.int32, logits.shape, 1)
    vals, idxs = [], []
    for _ in range(TOP_K):
        mx = jnp.max(logits, axis=-1, keepdims=True)
        ix = jnp.min(jnp.where(logits == mx, lane, LANE), axis=-1, keepdims=True)
        vals.append(mx)
        idxs.append(ix)
        logits = jnp.where(lane == ix, -jnp.inf, logits)
    es = [jnp.exp(v - vals[0]) for v in vals]
    inv = 1.0 / (es[0] + es[1] + es[2] + es[3])

    @pl.when(pl.program_id(0) == 0)
    def _():
        cnt_s[...] = jnp.zeros_like(cnt_s)

    chosen = jnp.zeros(lane.shape, F32)
    for k in range(TOP_K):
        chosen = jnp.where(lane == idxs[k], 1.0, chosen)
    earlier = (lax.broadcasted_iota(jnp.int32, (tm, tm), 0) > lax.broadcasted_iota(jnp.int32, (tm, tm), 1))
    before = jnp.dot(earlier.astype(BF16), chosen.astype(BF16), preferred_element_type=F32) + cnt_s[...]
    route_out = jnp.zeros(lane.shape, jnp.int32)
    wt_out = jnp.zeros(lane.shape, F32)
    for k in range(TOP_K):
        rank = jnp.sum(jnp.where(lane == idxs[k], before, 0.0), axis=-1, keepdims=True).astype(jnp.int32)
        route_out = jnp.where(lane == k, idxs[k], route_out)
        route_out = jnp.where(lane == TOP_K + k, rank, route_out)
        wt_out = jnp.where(lane == k, es[k] * inv, wt_out)
    route_ref[...] = route_out
    wt_ref[...] = wt_out
    total = cnt_s[...] + jnp.sum(chosen, axis=0, keepdims=True)
    cnt_s[...] = total
    cnt_ref[...] = total.astype(jnp.int32)


def out_router(merged, x, w_out, mod, norm2_g, router_w, router_b):
    tm = TM_ROUTE
    const = lambda shape: pl.BlockSpec(shape, lambda i: (0, 0))
    rw = jnp.zeros((D, LANE), F32).at[:, :N_EXPERTS].set(router_w)
    rb = jnp.full((1, LANE), -1e30, F32).at[0, :N_EXPERTS].set(router_b)
    return pl.pallas_call(
        functools.partial(_out_router_kernel, tm=tm),
        grid=(T // tm,),
        in_specs=[pl.BlockSpec((tm, D), lambda i: (i, 0)), pl.BlockSpec((tm, D), lambda i: (i, 0)),
                  const((D, D)), const((8, N_MOD * D)), const((1, D)), const((D, LANE)), const((1, LANE))],
        out_specs=[pl.BlockSpec((tm, D), lambda i: (i, 0)), pl.BlockSpec((tm * SLAB, LANE), lambda i: (i, 0)),
                   pl.BlockSpec((tm, LANE), lambda i: (i, 0)), pl.BlockSpec((tm, LANE), lambda i: (i, 0)),
                   const((1, LANE))],
        out_shape=[jax.ShapeDtypeStruct((T, D), F32), jax.ShapeDtypeStruct((T * SLAB, LANE), F32),
                   jax.ShapeDtypeStruct((T, LANE), jnp.int32), jax.ShapeDtypeStruct((T, LANE), F32),
                   jax.ShapeDtypeStruct((1, LANE), jnp.int32)],
        scratch_shapes=[pltpu.VMEM((1, LANE), F32)],
        compiler_params=_params("arbitrary"),
        name="out_router",
    )(merged, x, w_out, mod, norm2_g.reshape(1, D), rw, rb)


def _segments(counts):
    padded = ((counts + TM_MOE - 1) // TM_MOE) * TM_MOE
    seg_end = jnp.cumsum(padded)
    return padded, seg_end - padded, seg_end


def _step_plan(counts, n_col_blocks):
    padded, seg_start, seg_end = _segments(counts)
    n_tiles = seg_end[-1] // TM_MOE
    step = jnp.arange(NT_MOE * n_col_blocks, dtype=jnp.int32)
    tile_probe = jnp.minimum(step // n_col_blocks, n_tiles - 1)
    ex = jnp.minimum(jnp.searchsorted(seg_end, tile_probe * TM_MOE, side="right"), N_EXPERTS - 1).astype(jnp.int32)
    t0 = seg_start[ex] // TM_MOE
    ne = jnp.maximum(padded[ex] // TM_MOE, 1)
    valid = step < n_tiles * n_col_blocks
    local = jnp.where(valid, step - n_col_blocks * t0, n_col_blocks * ne - 1)
    col = local // ne
    tile = t0 + local % ne
    first = jnp.logical_and(valid, local % ne == 0)
    return (ex, col.astype(jnp.int32), tile.astype(jnp.int32), first.astype(jnp.int32), valid.astype(jnp.int32))


def _dispatch_kernel(pos_ref, h_hbm, xs_hbm, sem):
    i = pl.program_id(0)

    def row_copy(src, dst):
        return pltpu.make_async_copy(h_hbm.at[pl.ds(src, SLAB), :], xs_hbm.at[pl.ds(dst, SLAB), :], sem)

    def issue(t, carry):
        src = pl.multiple_of((i * TM_ROUTE + t) * SLAB, SLAB)
        for k in range(TOP_K):
            row_copy(src, pl.multiple_of(pos_ref[0, t * TOP_K + k] * SLAB, SLAB)).start()
        return carry

    lax.fori_loop(0, TM_ROUTE, issue, 0)
    n = TM_ROUTE * TOP_K * SLAB
    pltpu.make_async_copy(h_hbm.at[pl.ds(0, n), :], xs_hbm.at[pl.ds(0, n), :], sem).wait()


def dispatch(h2_slab, pos):
    n_steps = T // TM_ROUTE
    return pl.pallas_call(
        _dispatch_kernel,
        grid=(n_steps,),
        in_specs=[pl.BlockSpec((None, 1, TM_ROUTE * TOP_K), lambda i: (i, 0, 0), memory_space=pltpu.SMEM),
                  pl.BlockSpec(memory_space=pl.ANY)],
        out_specs=pl.BlockSpec(memory_space=pl.ANY),
        out_shape=jax.ShapeDtypeStruct((NPAD * SLAB, LANE), F32),
        scratch_shapes=[pltpu.SemaphoreType.DMA(())],
        compiler_params=_params("arbitrary"),
        name="moe_dispatch",
    )(pos.reshape(n_steps, 1, TM_ROUTE * TOP_K), h2_slab)


def _gate_up_kernel(ex_ref, col_ref, tile_ref, first_ref, valid_ref,
                    x_ref, wg_ref, wu_ref, bg_ref, bu_ref, o_ref, wg_s, wu_s):
    s = pl.program_id(0)

    @pl.when(first_ref[s] == 1)
    def _():
        wg_s[...] = wg_ref[...].astype(BF16)
        wu_s[...] = wu_ref[...].astype(BF16)

    @pl.when(valid_ref[s] == 1)
    def _():
        g = jnp.broadcast_to(bg_ref[...], (TM_MOE, TN_GU))
        u = jnp.broadcast_to(bu_ref[...], (TM_MOE, TN_GU))
        for c in range(SLAB // 2):
            xc = jnp.concatenate([x_ref[pl.ds(2 * c, TM_MOE, stride=SLAB), :],
                                  x_ref[pl.ds(2 * c + 1, TM_MOE, stride=SLAB), :]], axis=1).astype(BF16)
            ks = slice(c * 2 * LANE, (c + 1) * 2 * LANE)
            g = g + jnp.dot(xc, wg_s[ks, :], preferred_element_type=F32)
            u = u + jnp.dot(xc, wu_s[ks, :], preferred_element_type=F32)
        gate = jnp.minimum(g, SWIGLU_LIMIT)
        up = jnp.clip(u, -SWIGLU_LIMIT, SWIGLU_LIMIT)
        o_ref[...] = (gate * jax.nn.sigmoid(SWIGLU_ALPHA * gate) * (up + 1.0)).astype(o_ref.dtype)


def gate_up(x_sorted, w_gu, b_gu, layer, plan):
    ex, col, tile, first, valid = plan
    ncb = D_FF // TN_GU
    grid_spec = pltpu.PrefetchScalarGridSpec(
        num_scalar_prefetch=5,
        grid=(NT_MOE * ncb,),
        in_specs=[pl.BlockSpec((TM_MOE * SLAB, LANE), lambda s, ex, col, tile, first, valid: (tile[s], 0)),
                  pl.BlockSpec((None, None, D, TN_GU), lambda s, ex, col, tile, first, valid: (layer, ex[s], 0, col[s])),
                  pl.BlockSpec((None, None, D, TN_GU), lambda s, ex, col, tile, first, valid: (layer, ex[s], 0, ncb + col[s])),
                  pl.BlockSpec((None, None, 1, TN_GU), lambda s, ex, col, tile, first, valid: (layer, ex[s], 0, col[s])),
                  pl.BlockSpec((None, None, 1, TN_GU), lambda s, ex, col, tile, first, valid: (layer, ex[s], 0, ncb + col[s]))],
        out_specs=pl.BlockSpec((TM_MOE, TN_GU), lambda s, ex, col, tile, first, valid: (tile[s], col[s])),
        scratch_shapes=[pltpu.VMEM((D, TN_GU), BF16), pltpu.VMEM((D, TN_GU), BF16)],
    )
    b4 = b_gu.reshape(DEPTH, N_EXPERTS, 1, 2 * D_FF)
    return pl.pallas_call(
        _gate_up_kernel,
        grid_spec=grid_spec,
        out_shape=jax.ShapeDtypeStruct((NPAD, D_FF), BF16),
        compiler_params=_params("arbitrary"),
        name="moe_gate_up",
    )(ex, col, tile, first, valid, x_sorted, w_gu, w_gu, b4, b4)


def _down_kernel(ex_ref, col_ref, tile_ref, first_ref, valid_ref,
                 a_ref, w_ref, b_ref, o_ref, w_s):
    s = pl.program_id(0)

    @pl.when(first_ref[s] == 1)
    def _():
        w_s[...] = w_ref[...].astype(BF16)

    @pl.when(valid_ref[s] == 1)
    def _():
        y = jnp.dot(a_ref[...], w_s[...], preferred_element_type=F32) + b_ref[...]
        for r in range(YSLAB):
            o_ref[pl.ds(r, TM_MOE, stride=YSLAB), :] = y[:, r * LANE:(r + 1) * LANE]


def down(act, w_dn, b_dn, layer, plan):
    ex, col, tile, first, valid = plan
    ncb = D // TN_DN
    grid_spec = pltpu.PrefetchScalarGridSpec(
        num_scalar_prefetch=5,
        grid=(NT_MOE * ncb,),
        in_specs=[pl.BlockSpec((TM_MOE, D_FF), lambda s, ex, col, tile, first, valid: (tile[s], 0)),
                  pl.BlockSpec((None, None, D_FF, TN_DN), lambda s, ex, col, tile, first, valid: (layer, ex[s], 0, col[s])),
                  pl.BlockSpec((None, None, 1, TN_DN), lambda s, ex, col, tile, first, valid: (layer, ex[s], 0, col[s]))],
        out_specs=pl.BlockSpec((None, TM_MOE * YSLAB, LANE), lambda s, ex, col, tile, first, valid: (col[s], tile[s], 0)),
        scratch_shapes=[pltpu.VMEM((D_FF, TN_DN), BF16)],
    )
    return pl.pallas_call(
        _down_kernel,
        grid_spec=grid_spec,
        out_shape=jax.ShapeDtypeStruct((ncb, NPAD * YSLAB, LANE), F32),
        compiler_params=_params("arbitrary"),
        name="moe_down",
    )(ex, col, tile, first, valid, act, w_dn, b_dn.reshape(DEPTH, N_EXPERTS, 1, D))


def _combine_kernel(pos_ref, ys_hbm, x_ref, wt_ref, mod_ref, o_ref, buf, sem):
    i = pl.program_id(0)
    ncb = D // TN_DN
    tm = TM_ROUTE

    def row_copy(j, src, dst):
        return pltpu.make_async_copy(ys_hbm.at[j, pl.ds(src, YSLAB), :], buf.at[pl.ds(dst, YSLAB), :], sem)

    def issue(t, carry):
        for k in range(TOP_K):
            src = pl.multiple_of(pos_ref[0, t * TOP_K + k] * YSLAB, YSLAB)
            for j in range(ncb):
                row_copy(j, src, pl.multiple_of(((k * ncb + j) * tm + t) * YSLAB, YSLAB)).start()
        return carry

    lax.fori_loop(0, tm, issue, 0)
    pltpu.make_async_copy(ys_hbm.at[0, pl.ds(0, TOP_K * ncb * tm * YSLAB), :], buf, sem).wait()

    gate_row = mod_ref[pl.ds(_mod_row(i, tm), 1), 5 * D:6 * D]
    wk = [wt_ref[:, k:k + 1] for k in range(TOP_K)]
    for j in range(ncb):
        for s in range(YSLAB):
            acc = jnp.zeros((tm, LANE), F32)
            for k in range(TOP_K):
                acc = acc + wk[k] * buf[pl.ds((k * ncb + j) * tm * YSLAB + s, tm, stride=YSLAB), :]
            cs = slice(j * TN_DN + s * LANE, j * TN_DN + (s + 1) * LANE)
            o_ref[:, cs] = x_ref[:, cs] + gate_row[:, cs] * acc


def combine(ys, pos, wts, x1, mod):
    n_steps = T // TM_ROUTE
    ncb = D // TN_DN
    return pl.pallas_call(
        _combine_kernel,
        grid=(n_steps,),
        in_specs=[pl.BlockSpec((None, 1, TM_ROUTE * TOP_K), lambda i: (i, 0, 0), memory_space=pltpu.SMEM),
                  pl.BlockSpec(memory_space=pl.ANY),
                  pl.BlockSpec((TM_ROUTE, D), lambda i: (i, 0)),
                  pl.BlockSpec((TM_ROUTE, LANE), lambda i: (i, 0)),
                  pl.BlockSpec((8, N_MOD * D), lambda i: (0, 0))],
        out_specs=pl.BlockSpec((TM_ROUTE, D), lambda i: (i, 0)),
        out_shape=jax.ShapeDtypeStruct((T, D), F32),
        scratch_shapes=[pltpu.VMEM((TOP_K * ncb * TM_ROUTE * YSLAB, LANE), F32), pltpu.SemaphoreType.DMA(())],
        compiler_params=_params("arbitrary"),
        name="moe_combine",
    )(pos.reshape(n_steps, 1, TM_ROUTE * TOP_K), ys, x1, wts, mod)


def kernel(x_prompt, x_sample, cache_diff_k, cache_diff_v, cache_gqa_k, cache_gqa_v, c, c_ctx, w_mod, b_mod, norm1_g, norm2_g, w_in, hy_conv_w, hy_conv_b, hy_f_w1, hy_f_b1, hy_f_w2, hy_f_b2, hy_f_w3, hy_freq, hy_skip, df_qn_g, df_kn_g, df_lq1, df_lk1, df_lq2, df_lk2, df_subln_g, gq_qn_g, gq_kn_g, w_hy_o, w_df_o, w_gq_o, w_out, router_w, router_b, w_gu, b_gu, w_dn, b_dn):
    x = jnp.concatenate([x_prompt.reshape(T_CTX, D), x_sample.reshape(T_LAT, D)], axis=0)
    cond8 = jnp.zeros((8, D), F32).at[0].set(c_ctx).at[1:1 + DEC_BATCH].set(c)
    mods = adaln_all(cond8, w_mod, b_mod)

    cdk = cache_diff_k.reshape(DEC_BATCH, DEPTH, PAST, 512)
    cdv = cache_diff_v.reshape(DEC_BATCH, DEPTH, PAST, 512)
    cgk = cache_gqa_k.reshape(DEC_BATCH, DEPTH, PAST, 256)
    cgv = cache_gqa_v.reshape(DEC_BATCH, DEPTH, PAST, 256)

    new_dk, new_dv, new_gk, new_gv = [], [], [], []
    for l in range(DEPTH):
        mod = mods[l]
        h = modulate(x, norm1_g[l], mod, 0)
        p1 = proj(h, w_in, l, 0, C_GATE, F32, None, "proj_mix")
        gates = proj(h, w_in, l, C_GATE, 3 * D, BF16, "sigmoid", "proj_gates")

        tile128 = lambda g: jnp.tile(g, LANE // g.shape[0]).reshape(1, LANE)
        gains = (tile128(df_qn_g[l]), tile128(df_kn_g[l]), tile128(gq_qn_g[l]), tile128(gq_kn_g[l]),
                 tile128(df_subln_g[l]))
        lam_init = 0.8 - 0.6 * math.exp(-0.3 * l)
        lam = (jnp.exp(jnp.sum(df_lq1[l] * df_lk1[l])) - jnp.exp(jnp.sum(df_lq2[l] * df_lk2[l])) + lam_init)

        yb_c, yc_c, ndk, ndv, ngk, ngv = attention(p1, gains, lam, l)
        yb_l, yc_l = attention(p1, gains, lam, l, caches=(cdk[:, l], cdv[:, l], cgk[:, l], cgv[:, l]))
        new_dk.append(ndk.reshape(BATCH, SEQ, DF_HEADS, 2, DF_QK))
        new_dv.append(ndv.reshape(BATCH, SEQ, DF_HEADS, DF_V))
        new_gk.append(ngk.reshape(BATCH, SEQ, GQ_KV, GQ_DIM))
        new_gv.append(ngv.reshape(BATCH, SEQ, GQ_KV, GQ_DIM))

        filt = (hy_f_w1[l], hy_f_b1[l], hy_f_w2[l], hy_f_b2[l], hy_f_w3[l], hy_freq[l])
        ya_c = hyena(p1, hy_conv_w[l], hy_conv_b[l], hy_skip[l], _hyena_filter_spectra(SEQ, *filt),
                     SEQ, BATCH, 0, 512)
        ya_l = hyena(p1, hy_conv_w[l], hy_conv_b[l], hy_skip[l], _hyena_filter_spectra(DEC_SEQ, *filt),
                     DEC_SEQ, DEC_BATCH, T_CTX // DEC_SEQ, 256)

        ya = jnp.concatenate([ya_c, ya_l], axis=0)
        yb = jnp.concatenate([yb_c, yb_l], axis=0)
        yc = jnp.concatenate([yc_c, yc_l], axis=0)
        merged = merge(ya, yb, yc, gates, w_hy_o[l].astype(BF16), w_df_o[l].astype(BF16), w_gq_o[l].astype(BF16))
        x1, h2_slab, route, wts, cnt = out_router(merged, x, w_out[l].astype(BF16), mod, norm2_g[l],
                                                  router_w[l], router_b[l])

        counts = cnt[0, :N_EXPERTS]
        seg_start = _segments(counts)[1]
        pos = seg_start[route[:, :TOP_K]] + route[:, TOP_K:2 * TOP_K]
        x_sorted = dispatch(h2_slab, pos)
        act = gate_up(x_sorted, w_gu, b_gu, l, _step_plan(counts, D_FF // TN_GU))
        ys = down(act, w_dn, b_dn, l, _step_plan(counts, D // TN_DN))
        x = combine(ys, pos, wts, x1, mod)

    y_prompt = x[:T_CTX].reshape(BATCH, SEQ, D)
    y_sample = x[T_CTX:].reshape(DEC_BATCH, DEC_SEQ, D)
    return (y_prompt, y_sample, jnp.stack(new_dk, axis=1), jnp.stack(new_dv, axis=1),
            jnp.stack(new_gk, axis=1), jnp.stack(new_gv, axis=1))
```

```python
import functools
import math

import jax
import jax.numpy as jnp
import numpy as np
from jax import lax
from jax.experimental import pallas as pl
from jax.experimental.pallas import tpu as pltpu

F32 = jnp.float32
BF16 = jnp.bfloat16

D = 2048
BATCH, SEQ = 32, 256
DEPTH = 2
DEC_BATCH, DEC_SEQ = 2, 1024
PAST = 256
GRID_W = 64
ROPE_THETA = 10000.0
EPS = 1e-6
N_MOD = 6
HY_CH = 512
HY_BANDS = 16
HY_FFN = 64
DF_HEADS, DF_QK, DF_V = 4, 64, 128
GQ_HEADS, GQ_KV, GQ_DIM = 8, 2, 128
N_EXPERTS, TOP_K = 32, 4
D_FF = D
SWIGLU_LIMIT = 7.0
SWIGLU_ALPHA = 1.702

T_CTX = BATCH * SEQ
T_LAT = DEC_BATCH * DEC_SEQ
T = T_CTX + T_LAT
N_ASSIGN = T * TOP_K

C_HY = 0
C_DQ = 3 * HY_CH
C_DK = C_DQ + 512
C_DV = C_DK + 512
C_GQ = C_DV + 512
C_GK = C_GQ + 1024
C_GV = C_GK + 256
C_GATE = C_GV + 256
D_IN = C_GATE + 3 * D

LANE = 128
VMEM_LIMIT = 56 * 1024 * 1024

TM_TOK = 512
TM_PROJ = 1024
TN_PROJ = 512
QC = 256
TM_MOE = 512
TN_GU = 512
TM_DN = 256
SLAB = 16
TM_ROUTE = 256
NT_MOE = N_ASSIGN // TM_MOE + N_EXPERTS
NPAD = NT_MOE * TM_MOE


def _params(*sem):
    return pltpu.CompilerParams(dimension_semantics=sem, vmem_limit_bytes=VMEM_LIMIT)


def _mod_row(tile_idx, tm):
    tok0 = tile_idx * tm
    return jnp.where(tok0 < T_CTX, 0, 1 + (tok0 - T_CTX) // DEC_SEQ)


def _adaln_kernel(c_ref, w_ref, b_ref, o_ref):
    c = c_ref[...]
    a = (c * jax.nn.sigmoid(c)).astype(BF16)
    o_ref[...] = jnp.dot(a, w_ref[...].astype(BF16), preferred_element_type=F32) + b_ref[...]


def adaln_all(cond8, w_mod, b_mod):
    tn = 1024
    return pl.pallas_call(
        _adaln_kernel,
        grid=(DEPTH, N_MOD * D // tn),
        in_specs=[pl.BlockSpec((8, D), lambda l, j: (0, 0)),
                  pl.BlockSpec((None, D, tn), lambda l, j: (l, 0, j)),
                  pl.BlockSpec((None, 1, tn), lambda l, j: (l, 0, j))],
        out_specs=pl.BlockSpec((None, 8, tn), lambda l, j: (l, 0, j)),
        out_shape=jax.ShapeDtypeStruct((DEPTH, 8, N_MOD * D), F32),
        compiler_params=_params("arbitrary", "arbitrary"),
        name="adaln",
    )(cond8, w_mod, b_mod.reshape(DEPTH, 1, N_MOD * D))


def _modulated_norm(x, g, sh, sc):
    ms = jnp.mean(x * x, axis=-1, keepdims=True)
    return (x * lax.rsqrt(ms + EPS) * g) * (1.0 + sc) + sh


def _modulate_kernel(x_ref, g_ref, mod_ref, o_ref, *, k_shift):
    r = _mod_row(pl.program_id(0), TM_TOK)
    sh = mod_ref[pl.ds(r, 1), k_shift * D:(k_shift + 1) * D]
    sc = mod_ref[pl.ds(r, 1), (k_shift + 1) * D:(k_shift + 2) * D]
    o_ref[...] = _modulated_norm(x_ref[...], g_ref[...], sh, sc).astype(BF16)


def modulate(x, g, mod, k_shift):
    return pl.pallas_call(
        functools.partial(_modulate_kernel, k_shift=k_shift),
        grid=(T // TM_TOK,),
        in_specs=[pl.BlockSpec((TM_TOK, D), lambda i: (i, 0)),
                  pl.BlockSpec((1, D), lambda i: (0, 0)),
                  pl.BlockSpec((8, N_MOD * D), lambda i: (0, 0))],
        out_specs=pl.BlockSpec((TM_TOK, D), lambda i: (i, 0)),
        out_shape=jax.ShapeDtypeStruct((T, D), BF16),
        compiler_params=_params("arbitrary"),
        name="modulate",
    )(x, g.reshape(1, D), mod)


def _proj_kernel(x_ref, w_ref, o_ref, wbf_ref, *, act):
    @pl.when(pl.program_id(1) == 0)
    def _():
        wbf_ref[...] = w_ref[...].astype(BF16)

    y = jnp.dot(x_ref[...], wbf_ref[...], preferred_element_type=F32)
    if act == "sigmoid":
        y = jax.nn.sigmoid(y)
    o_ref[...] = y.astype(o_ref.dtype)


def proj(h, w, layer, col0, ncols, out_dtype, act, name):
    blk0 = col0 // TN_PROJ
    return pl.pallas_call(
        functools.partial(_proj_kernel, act=act),
        grid=(ncols // TN_PROJ, T // TM_PROJ),
        in_specs=[pl.BlockSpec((TM_PROJ, D), lambda j, i: (i, 0)),
                  pl.BlockSpec((None, D, TN_PROJ), lambda j, i: (layer, 0, blk0 + j))],
        out_specs=pl.BlockSpec((TM_PROJ, TN_PROJ), lambda j, i: (i, j)),
        out_shape=jax.ShapeDtypeStruct((T, ncols), out_dtype),
        scratch_shapes=[pltpu.VMEM((D, TN_PROJ), BF16)],
        compiler_params=_params("arbitrary", "arbitrary"),
        name=name,
    )(h, w)


def _norm_halves(x, g):
    lo = lax.broadcasted_iota(jnp.int32, x.shape, 1) < DF_QK
    x2 = x * x
    s_lo = jnp.sum(jnp.where(lo, x2, 0.0), axis=-1, keepdims=True)
    s_hi = jnp.sum(jnp.where(lo, 0.0, x2), axis=-1, keepdims=True)
    inv = jnp.where(lo, lax.rsqrt(s_lo * (1.0 / DF_QK) + EPS), lax.rsqrt(s_hi * (1.0 / DF_QK) + EPS))
    return x * inv * g


def _norm_full(x, g):
    ms = jnp.mean(x * x, axis=-1, keepdims=True)
    return x * lax.rsqrt(ms + EPS) * g


def _rope(x, cos, sin_a, sin_b, shift):
    return x * cos + pltpu.roll(x, LANE - shift, 1) * sin_a + pltpu.roll(x, shift, 1) * sin_b


def _softmax(s):
    m = jnp.max(s, axis=-1, keepdims=True)
    e = jnp.exp(s - m)
    return e / jnp.sum(e, axis=-1, keepdims=True)


def _dot_nt(a, b):
    return lax.dot_general(a, b, (((1,), (1,)), ((), ())), preferred_element_type=F32)


def _attn_kernel(*refs, lq, has_ctx, out_scale):
    if has_ctx:
        (dq_ref, dk_ref, dv_ref, gq_ref, gk_ref, gv_ref, cdk_ref, cdv_ref, cgk_ref, cgv_ref,
         cd_ref, sad_ref, sbd_ref, cg_ref, sag_ref, sbg_ref, cdq_ref, sadq_ref, sbdq_ref,
         cgq_ref, sagq_ref, sbgq_ref,
         qn_ref, kn_ref, gqn_ref, gkn_ref, sub_ref, lam_ref,
         yb_ref, yc_ref, kd_s, vd_s, kg_s, vg_s) = refs
    else:
        (dq_ref, dk_ref, dv_ref, gq_ref, gk_ref, gv_ref,
         qn_ref, kn_ref, gqn_ref, gkn_ref, sub_ref, lam_ref,
         yb_ref, yc_ref, ndk_ref, ndv_ref, ngk_ref, ngv_ref, kd_s, vd_s, kg_s, vg_s) = refs

    @pl.when(pl.program_id(1) == 0)
    def _():
        for h in range(DF_HEADS):
            cs = slice(h * LANE, (h + 1) * LANE)
            kn = _norm_halves(dk_ref[:, cs], kn_ref[...])
            if has_ctx:
                kn = _rope(kn, cd_ref[...], sad_ref[...], sbd_ref[...], DF_QK // 4)
            else:
                ndk_ref[:, cs] = kn
            kd_s[0:lq, cs] = kn.astype(BF16)
        vd_s[0:lq, :] = dv_ref[...].astype(BF16)
        for g in range(GQ_KV):
            cs = slice(g * LANE, (g + 1) * LANE)
            kn = _norm_full(gk_ref[:, cs], gkn_ref[...])
            if has_ctx:
                kn = _rope(kn, cg_ref[...], sag_ref[...], sbg_ref[...], GQ_DIM // 4)
            else:
                ngk_ref[:, cs] = kn
            kg_s[0:lq, cs] = kn.astype(BF16)
        vg_s[0:lq, :] = gv_ref[...].astype(BF16)
        if has_ctx:
            kd_s[lq:lq + PAST, :] = cdk_ref[...].astype(BF16)
            vd_s[lq:lq + PAST, :] = cdv_ref[...].astype(BF16)
            kg_s[lq:lq + PAST, :] = cgk_ref[...].astype(BF16)
            vg_s[lq:lq + PAST, :] = cgv_ref[...].astype(BF16)
        else:
            ndv_ref[...] = dv_ref[...]
            ngv_ref[...] = gv_ref[...]

    lam = lam_ref[:, 0:1]
    lo = lax.broadcasted_iota(jnp.int32, (QC, LANE), 1) < DF_QK

    for h in range(DF_HEADS):
        cs = slice(h * LANE, (h + 1) * LANE)
        qn = _norm_halves(dq_ref[:, cs], qn_ref[...])
        if has_ctx:
            qn = _rope(qn, cdq_ref[...], sadq_ref[...], sbdq_ref[...], DF_QK // 4)
        qn = qn * (DF_QK ** -0.5)
        kh = kd_s[:, cs]
        p0 = _softmax(_dot_nt(jnp.where(lo, qn, 0.0).astype(BF16), kh))
        p1 = _softmax(_dot_nt(jnp.where(lo, 0.0, qn).astype(BF16), kh))
        w = (p0 - lam * p1).astype(BF16)
        o = jnp.dot(w, vd_s[:, cs], preferred_element_type=F32)
        yb_ref[:, cs] = (_norm_full(o, sub_ref[...]) * out_scale).astype(yb_ref.dtype)

    for hq in range(GQ_HEADS):
        g = hq // (GQ_HEADS // GQ_KV)
        cs = slice(hq * LANE, (hq + 1) * LANE)
        ks = slice(g * LANE, (g + 1) * LANE)
        qn = _norm_full(gq_ref[:, cs], gqn_ref[...])
        if has_ctx:
            qn = _rope(qn, cgq_ref[...], sagq_ref[...], sbgq_ref[...], GQ_DIM // 4)
        s = _dot_nt(qn.astype(BF16), kg_s[:, ks]) * (GQ_DIM ** -0.5)
        p = _softmax(s).astype(BF16)
        yc_ref[:, cs] = jnp.dot(p, vg_s[:, ks], preferred_element_type=F32).astype(yc_ref.dtype)


def _rope_tables(d_head):
    t = np.arange(DEC_SEQ)
    row, col = (t // GRID_W).astype(np.float64), (t % GRID_W).astype(np.float64)
    q = d_head // 4
    inv = ROPE_THETA ** (-np.arange(q, dtype=np.float64) / q)
    ar, ac = row[:, None] * inv, col[:, None] * inv
    z = np.zeros_like(ar)
    cos = np.concatenate([np.cos(ar), np.cos(ar), np.cos(ac), np.cos(ac)], -1)
    sin_a = np.concatenate([-np.sin(ar), z, -np.sin(ac), z], -1)
    sin_b = np.concatenate([z, np.sin(ar), z, np.sin(ac)], -1)
    rep = LANE // d_head
    return tuple(jnp.asarray(np.tile(a, (1, rep)), F32) for a in (cos, sin_a, sin_b))


def attention(p1, gains, lam, layer_idx, caches=None):
    has_ctx = caches is not None
    lq = DEC_SEQ if has_ctx else SEQ
    n_seq = DEC_BATCH if has_ctx else BATCH
    row0 = (T_CTX // lq) if has_ctx else 0
    lk = lq + (PAST if has_ctx else 0)
    nq = lq // QC
    lam_init = 0.8 - 0.6 * math.exp(-0.3 * layer_idx)

    def qspec(width, col0):
        return pl.BlockSpec((QC, width), lambda b, q: ((row0 + b) * nq + q, col0 // width))

    def kspec(width, col0):
        return pl.BlockSpec((lq, width), lambda b, q: (row0 + b, col0 // width))

    const = lambda shape: pl.BlockSpec(shape, lambda b, q: (0,) * len(shape))
    in_specs = [qspec(512, C_DQ), kspec(512, C_DK), kspec(512, C_DV),
                qspec(1024, C_GQ), kspec(256, C_GK), kspec(256, C_GV)]
    args = [p1] * 6
    if has_ctx:
        cdk, cdv, cgk, cgv = caches
        in_specs += [pl.BlockSpec((None, PAST, 512), lambda b, q: (b, 0, 0)),
                     pl.BlockSpec((None, PAST, 512), lambda b, q: (b, 0, 0)),
                     pl.BlockSpec((None, PAST, 256), lambda b, q: (b, 0, 0)),
                     pl.BlockSpec((None, PAST, 256), lambda b, q: (b, 0, 0))]
        args += [cdk, cdv, cgk, cgv]
        td, tg = _rope_tables(DF_QK), _rope_tables(GQ_DIM)
        in_specs += [const((lq, LANE))] * 6 + [pl.BlockSpec((QC, LANE), lambda b, q: (q, 0))] * 6
        args += list(td) + list(tg) + list(td) + list(tg)
    in_specs += [const((1, LANE))] * 6
    args += list(gains) + [jnp.full((1, LANE), lam, F32)]

    out_specs = [pl.BlockSpec((QC, 512), lambda b, q: ((row0 + b) * nq + q - row0 * nq, 0)),
                 pl.BlockSpec((QC, 1024), lambda b, q: ((row0 + b) * nq + q - row0 * nq, 0))]
    n_rows = n_seq * lq
    out_shape = [jax.ShapeDtypeStruct((n_rows, 512), BF16), jax.ShapeDtypeStruct((n_rows, 1024), BF16)]
    if not has_ctx:
        out_specs += [pl.BlockSpec((lq, 512), lambda b, q: (b, 0)), pl.BlockSpec((lq, 512), lambda b, q: (b, 0)),
                      pl.BlockSpec((lq, 256), lambda b, q: (b, 0)), pl.BlockSpec((lq, 256), lambda b, q: (b, 0))]
        out_shape += [jax.ShapeDtypeStruct((n_rows, 512), F32), jax.ShapeDtypeStruct((n_rows, 512), F32),
                      jax.ShapeDtypeStruct((n_rows, 256), F32), jax.ShapeDtypeStruct((n_rows, 256), F32)]
    return pl.pallas_call(
        functools.partial(_attn_kernel, lq=lq, has_ctx=has_ctx, out_scale=1.0 - lam_init),
        grid=(n_seq, nq),
        in_specs=in_specs,
        out_specs=out_specs,
        out_shape=out_shape,
        scratch_shapes=[pltpu.VMEM((lk, 512), BF16), pltpu.VMEM((lk, 512), BF16),
                        pltpu.VMEM((lk, 256), BF16), pltpu.VMEM((lk, 256), BF16)],
        compiler_params=_params("arbitrary", "arbitrary"),
        name="attn_latent" if has_ctx else "attn_context",
    )(*args)


def _short_conv(u, w_ref, b_ref, length):
    row = lax.broadcasted_iota(jnp.int32, u.shape, 0)
    prev = jnp.where(row == 0, 0.0, pltpu.roll(u, 1, 0))
    nxt = jnp.where(row == length - 1, 0.0, pltpu.roll(u, length - 1, 0))
    return w_ref[0:1, :] * prev + w_ref[1:2, :] * u + w_ref[2:3, :] * nxt + b_ref[...]


def _hyena_kernel(v_ref, x_ref, cwv_ref, cbv_ref, cwx_ref, cbx_ref, skip_ref, fm_ref, gm_ref,
                  a_ref, b_ref, a2_ref, o_ref, z_s, *, length):
    n = pl.program_id(2)

    @pl.when(n == 0)
    def _():
        z_s[...] = _short_conv(v_ref[...], cwv_ref, cbv_ref, length)

    z = z_s[...]
    xn = _short_conv(x_ref[...], cwx_ref, cbx_ref, length)
    u = jnp.dot(fm_ref[...], z.astype(BF16), preferred_element_type=F32)
    ur, ui = u[:length], u[length:]
    a, b, a2 = a_ref[...], b_ref[...], a2_ref[...]
    yr = (ur * a - ui * b).astype(BF16)
    yi = (ur * b + ui * a2).astype(BF16)
    conv = (jnp.dot(gm_ref[:, :length], yr, preferred_element_type=F32)
            + jnp.dot(gm_ref[:, length:], yi, preferred_element_type=F32))
    z = xn * (conv + skip_ref[...] * z)
    z_s[...] = z
    o_ref[...] = z.astype(o_ref.dtype)


def _dft_matrices(length):
    n = 2 * length
    f = np.arange(length)[:, None]
    t = np.arange(length)[None, :]
    ang = ((f * t) % n).astype(np.float64) * (2.0 * math.pi / n)
    cos, sin = np.cos(ang), np.sin(ang)
    alt = np.broadcast_to(np.where(t % 2 == 0, 1.0, -1.0), sin.shape)
    fm = np.concatenate([cos, np.where(f == 0, alt, -sin)], axis=0)
    wre = np.where(f == 0, 1.0 / n, 2.0 / n) * cos
    wim = np.where(f == 0, alt / n, -(2.0 / n) * sin)
    gm = np.concatenate([wre, wim], axis=0).T
    return jnp.asarray(fm, BF16), jnp.asarray(gm, BF16)


def _hyena_filter_spectra(length, w1, b1, w2, b2, w3, freq):
    hp = lax.Precision.HIGHEST
    t = jnp.linspace(0.0, 1.0, length, dtype=F32)[:, None]
    bands = jnp.linspace(1e-4, HY_BANDS - 1, HY_BANDS, dtype=F32)
    wpos = (2 * math.pi / length) * jnp.arange(length, dtype=F32)[:, None] * bands
    z = jnp.concatenate([t, jnp.cos(wpos), -jnp.sin(wpos)], -1)
    h = jnp.sin(freq * (jnp.dot(z, w1, precision=hp) + b1))
    h = jnp.sin(freq * (jnp.dot(h, w2, precision=hp) + b2))
    h = jnp.dot(h, w3, precision=hp).reshape(length, 2, 2, HY_CH)
    max_decay = math.log(1e-2) / 0.3
    min_decay = math.log(1e-2) / 1.5
    deltas = jnp.linspace(min_decay, max_decay, HY_CH, dtype=F32)
    h = h * jnp.exp(-t[:, :, None, None] * jnp.abs(deltas))
    fwd, bwd = h[:, :, 0], h[:, :, 1]
    k = jnp.concatenate([fwd, jnp.zeros((1, 2, HY_CH), F32), jnp.flip(bwd[1:], 0)], 0)
    kf = jnp.fft.rfft(k, axis=0)
    a = jnp.moveaxis(jnp.real(kf[:length]), 1, 0)
    b = jnp.moveaxis(jnp.imag(kf[:length]), 1, 0).at[:, 0].set(0.0)
    a2 = a.at[:, 0].set(jnp.real(kf[length]))
    return a, b, a2


def hyena(p1, conv_w, conv_b, skip, spectra, length, n_seq, row0, cw):
    fm, gm = _dft_matrices(length)
    a, b, a2 = spectra
    ncb = HY_CH // cw
    vblk = 2 * ncb
    const = lambda shape: pl.BlockSpec(shape, lambda s, c, n: (0,) * len(shape))
    spec_spec = pl.BlockSpec((None, length, cw), lambda s, c, n: (n, 0, c))
    return pl.pallas_call(
        functools.partial(_hyena_kernel, length=length),
        grid=(n_seq, ncb, 2),
        in_specs=[pl.BlockSpec((length, cw), lambda s, c, n: (row0 + s, vblk + c)),
                  pl.BlockSpec((length, cw), lambda s, c, n: (row0 + s, n * ncb + c)),
                  pl.BlockSpec((3, cw), lambda s, c, n: (0, vblk + c)),
                  pl.BlockSpec((1, cw), lambda s, c, n: (0, vblk + c)),
                  pl.BlockSpec((3, cw), lambda s, c, n: (0, n * ncb + c)),
                  pl.BlockSpec((1, cw), lambda s, c, n: (0, n * ncb + c)),
                  pl.BlockSpec((None, 1, cw), lambda s, c, n: (n, 0, c)),
                  const((2 * length, length)), const((length, 2 * length)),
                  spec_spec, spec_spec, spec_spec],
        out_specs=pl.BlockSpec((length, cw), lambda s, c, n: (s, c)),
        out_shape=jax.ShapeDtypeStruct((n_seq * length, HY_CH), BF16),
        scratch_shapes=[pltpu.VMEM((length, cw), F32)],
        compiler_params=_params("arbitrary", "arbitrary", "arbitrary"),
        name=f"hyena_{length}",
    )(p1, p1, conv_w, conv_b.reshape(1, -1), conv_w, conv_b.reshape(1, -1),
      skip.reshape(2, 1, HY_CH), fm, gm, a, b, a2)


def _merge_kernel(ya_ref, yb_ref, yc_ref, g_ref, wa_ref, wb_ref, wc_ref, o_ref):
    a = jnp.dot(ya_ref[...], wa_ref[...], preferred_element_type=F32)
    b = jnp.dot(yb_ref[...], wb_ref[...], preferred_element_type=F32)
    c = jnp.dot(yc_ref[...], wc_ref[...], preferred_element_type=F32)
    m = g_ref[:, 0:D] * a + g_ref[:, D:2 * D] * b + g_ref[:, 2 * D:3 * D] * c
    o_ref[...] = m.astype(o_ref.dtype)


def merge(ya, yb, yc, gates, wa, wb, wc):
    tm = 256
    const = lambda shape: pl.BlockSpec(shape, lambda i: (0, 0))
    return pl.pallas_call(
        _merge_kernel,
        grid=(T // tm,),
        in_specs=[pl.BlockSpec((tm, 512), lambda i: (i, 0)), pl.BlockSpec((tm, 512), lambda i: (i, 0)),
                  pl.BlockSpec((tm, 1024), lambda i: (i, 0)), pl.BlockSpec((tm, 3 * D), lambda i: (i, 0)),
                  const((512, D)), const((512, D)), const((1024, D))],
        out_specs=pl.BlockSpec((tm, D), lambda i: (i, 0)),
        out_shape=jax.ShapeDtypeStruct((T, D), BF16),
        compiler_params=_params("arbitrary"),
        name="merge",
    )(ya, yb, yc, gates, wa, wb, wc)


def _split3(x):
    hi = x.astype(BF16)
    return hi, (x - hi.astype(F32)).astype(BF16)


def _out_router_kernel(m_ref, x_ref, wout_ref, mod_ref, g_ref, rw_ref, rb_ref,
                       x1_ref, h2_ref, route_ref, wt_ref, cnt_ref, cnt_s, *, tm):
    r = _mod_row(pl.program_id(0), tm)
    o = jnp.dot(m_ref[...], wout_ref[...], preferred_element_type=F32)
    x1 = x_ref[...] + mod_ref[pl.ds(r, 1), 2 * D:3 * D] * o
    x1_ref[...] = x1
    h2 = _modulated_norm(x1, g_ref[...], mod_ref[pl.ds(r, 1), 3 * D:4 * D], mod_ref[pl.ds(r, 1), 4 * D:5 * D])
    for s in range(SLAB):
        h2_ref[pl.ds(s, tm, stride=SLAB), :] = h2[:, s * LANE:(s + 1) * LANE]

    h_hi, h_lo = _split3(h2)
    w_hi, w_lo = _split3(rw_ref[...])
    logits = (jnp.dot(h_hi, w_hi, preferred_element_type=F32) + jnp.dot(h_hi, w_lo, preferred_element_type=F32)
              + jnp.dot(h_lo, w_hi, preferred_element_type=F32)) + rb_ref[...]

    lane = lax.broadcasted_iota(jnp.int32, logits.shape, 1)
    vals, idxs = [], []
    for _ in range(TOP_K):
        mx = jnp.max(logits, axis=-1, keepdims=True)
        ix = jnp.min(jnp.where(logits == mx, lane, LANE), axis=-1, keepdims=True)
        vals.append(mx)
        idxs.append(ix)
        logits = jnp.where(lane == ix, -jnp.inf, logits)
    es = [jnp.exp(v - vals[0]) for v in vals]
    inv = 1.0 / (es[0] + es[1] + es[2] + es[3])

    @pl.when(pl.program_id(0) == 0)
    def _():
        cnt_s[...] = jnp.zeros_like(cnt_s)

    chosen = jnp.zeros(lane.shape, F32)
    for k in range(TOP_K):
        chosen = jnp.where(lane == idxs[k], 1.0, chosen)
    earlier = (lax.broadcasted_iota(jnp.int32, (tm, tm), 0) > lax.broadcasted_iota(jnp.int32, (tm, tm), 1))
    before = jnp.dot(earlier.astype(BF16), chosen.astype(BF16), preferred_element_type=F32) + cnt_s[...]
    route_out = jnp.zeros(lane.shape, jnp.int32)
    wt_out = jnp.zeros(lane.shape, F32)
    for k in range(TOP_K):
        rank = jnp.sum(jnp.where(lane == idxs[k], before, 0.0), axis=-1, keepdims=True).astype(jnp.int32)
        route_out = jnp.where(lane == k, idxs[k], route_out)
        route_out = jnp.where(lane == TOP_K + k, rank, route_out)
        wt_out = jnp.where(lane == k, es[k] * inv, wt_out)
    route_ref[...] = route_out
    wt_ref[...] = wt_out
    total = cnt_s[...] + jnp.sum(chosen, axis=0, keepdims=True)
    cnt_s[...] = total
    cnt_ref[...] = total.astype(jnp.int32)


def out_router(merged, x, w_out, mod, norm2_g, router_w, router_b):
    tm = TM_ROUTE
    const = lambda shape: pl.BlockSpec(shape, lambda i: (0, 0))
    rw = jnp.zeros((D, LANE), F32).at[:, :N_EXPERTS].set(router_w)
    rb = jnp.full((1, LANE), -1e30, F32).at[0, :N_EXPERTS].set(router_b)
    return pl.pallas_call(
        functools.partial(_out_router_kernel, tm=tm),
        grid=(T // tm,),
        in_specs=[pl.BlockSpec((tm, D), lambda i: (i, 0)), pl.BlockSpec((tm, D), lambda i: (i, 0)),
                  const((D, D)), const((8, N_MOD * D)), const((1, D)), const((D, LANE)), const((1, LANE))],
        out_specs=[pl.BlockSpec((tm, D), lambda i: (i, 0)), pl.BlockSpec((tm * SLAB, LANE), lambda i: (i, 0)),
                   pl.BlockSpec((tm, LANE), lambda i: (i, 0)), pl.BlockSpec((tm, LANE), lambda i: (i, 0)),
                   const((1, LANE))],
        out_shape=[jax.ShapeDtypeStruct((T, D), F32), jax.ShapeDtypeStruct((T * SLAB, LANE), F32),
                   jax.ShapeDtypeStruct((T, LANE), jnp.int32), jax.ShapeDtypeStruct((T, LANE), F32),
                   jax.ShapeDtypeStruct((1, LANE), jnp.int32)],
        scratch_shapes=[pltpu.VMEM((1, LANE), F32)],
        compiler_params=_params("arbitrary"),
        name="out_router",
    )(merged, x, w_out, mod, norm2_g.reshape(1, D), rw, rb)


def _segments(counts):
    padded = ((counts + TM_MOE - 1) // TM_MOE) * TM_MOE
    seg_end = jnp.cumsum(padded)
    return padded, seg_end - padded, seg_end


def _step_plan(counts, n_col_blocks, tm):
    padded, seg_start, seg_end = _segments(counts)
    n_tiles = seg_end[-1] // tm
    step = jnp.arange((NPAD // tm) * n_col_blocks, dtype=jnp.int32)
    tile_probe = jnp.minimum(step // n_col_blocks, n_tiles - 1)
    ends_before = (seg_end[None, :] <= (tile_probe * tm)[:, None]).astype(jnp.int32)
    ex = jnp.minimum(jnp.sum(ends_before, axis=1), N_EXPERTS - 1)
    onehot = (ex[:, None] == jnp.arange(N_EXPERTS, dtype=jnp.int32)[None, :]).astype(jnp.int32)
    t0 = jnp.sum(onehot * seg_start[None, :], axis=1) // tm
    ne = jnp.maximum(jnp.sum(onehot * padded[None, :], axis=1) // tm, 1)
    valid = step < n_tiles * n_col_blocks
    local = jnp.where(valid, step - n_col_blocks * t0, n_col_blocks * ne - 1)
    col = local // ne
    tile = t0 + local % ne
    first = jnp.logical_and(valid, local % ne == 0)
    return (ex, col.astype(jnp.int32), tile.astype(jnp.int32), first.astype(jnp.int32), valid.astype(jnp.int32))


def _dispatch_kernel(pos_ref, h_ref, xs_hbm, sem):
    def issue(t, carry):
        src = pl.multiple_of(t * SLAB, SLAB)
        for k in range(TOP_K):
            dst = pl.multiple_of(pos_ref[0, t * TOP_K + k] * SLAB, SLAB)
            pltpu.make_async_copy(h_ref.at[pl.ds(src, SLAB), :], xs_hbm.at[pl.ds(dst, SLAB), :], sem).start()
        return carry

    lax.fori_loop(0, TM_ROUTE, issue, 0)
    for _ in range(TOP_K):
        pltpu.make_async_copy(h_ref, xs_hbm.at[pl.ds(0, TM_ROUTE * SLAB), :], sem).wait()


def dispatch(h2_slab, pos):
    n_steps = T // TM_ROUTE
    return pl.pallas_call(
        _dispatch_kernel,
        grid=(n_steps,),
        in_specs=[pl.BlockSpec((None, 1, TM_ROUTE * TOP_K), lambda i: (i, 0, 0), memory_space=pltpu.SMEM),
                  pl.BlockSpec((TM_ROUTE * SLAB, LANE), lambda i: (i, 0))],
        out_specs=pl.BlockSpec(memory_space=pl.ANY),
        out_shape=jax.ShapeDtypeStruct((NPAD * SLAB, LANE), F32),
        scratch_shapes=[pltpu.SemaphoreType.DMA(())],
        compiler_params=_params("arbitrary"),
        name="moe_dispatch",
    )(pos.reshape(n_steps, 1, TM_ROUTE * TOP_K), h2_slab)


def _gate_up_kernel(ex_ref, col_ref, tile_ref, first_ref, valid_ref,
                    x_ref, wg_ref, wu_ref, bg_ref, bu_ref, o_ref, wg_s, wu_s):
    s = pl.program_id(0)

    @pl.when(first_ref[s] == 1)
    def _():
        wg_s[...] = wg_ref[...].astype(BF16)
        wu_s[...] = wu_ref[...].astype(BF16)

    @pl.when(valid_ref[s] == 1)
    def _():
        g = jnp.broadcast_to(bg_ref[...], (TM_MOE, TN_GU))
        u = jnp.broadcast_to(bu_ref[...], (TM_MOE, TN_GU))
        for c in range(SLAB // 2):
            xc = jnp.concatenate([x_ref[pl.ds(2 * c, TM_MOE, stride=SLAB), :],
                                  x_ref[pl.ds(2 * c + 1, TM_MOE, stride=SLAB), :]], axis=1).astype(BF16)
            ks = slice(c * 2 * LANE, (c + 1) * 2 * LANE)
            g = g + jnp.dot(xc, wg_s[ks, :], preferred_element_type=F32)
            u = u + jnp.dot(xc, wu_s[ks, :], preferred_element_type=F32)
        gate = jnp.minimum(g, SWIGLU_LIMIT)
        up = jnp.clip(u, -SWIGLU_LIMIT, SWIGLU_LIMIT)
        o_ref[...] = (gate * jax.nn.sigmoid(SWIGLU_ALPHA * gate) * (up + 1.0)).astype(o_ref.dtype)


def gate_up(x_sorted, w_gu, b_gu, layer, plan):
    ex, col, tile, first, valid = plan
    ncb = D_FF // TN_GU
    grid_spec = pltpu.PrefetchScalarGridSpec(
        num_scalar_prefetch=5,
        grid=(NT_MOE * ncb,),
        in_specs=[pl.BlockSpec((TM_MOE * SLAB, LANE), lambda s, ex, col, tile, first, valid: (tile[s], 0)),
                  pl.BlockSpec((None, None, D, TN_GU), lambda s, ex, col, tile, first, valid: (layer, ex[s], 0, col[s])),
                  pl.BlockSpec((None, None, D, TN_GU), lambda s, ex, col, tile, first, valid: (layer, ex[s], 0, ncb + col[s])),
                  pl.BlockSpec((None, None, 1, TN_GU), lambda s, ex, col, tile, first, valid: (layer, ex[s], 0, col[s])),
                  pl.BlockSpec((None, None, 1, TN_GU), lambda s, ex, col, tile, first, valid: (layer, ex[s], 0, ncb + col[s]))],
        out_specs=pl.BlockSpec((TM_MOE, TN_GU), lambda s, ex, col, tile, first, valid: (tile[s], col[s])),
        scratch_shapes=[pltpu.VMEM((D, TN_GU), BF16), pltpu.VMEM((D, TN_GU), BF16)],
    )
    b4 = b_gu.reshape(DEPTH, N_EXPERTS, 1, 2 * D_FF)
    return pl.pallas_call(
        _gate_up_kernel,
        grid_spec=grid_spec,
        out_shape=jax.ShapeDtypeStruct((NPAD, D_FF), BF16),
        compiler_params=_params("arbitrary"),
        name="moe_gate_up",
    )(ex, col, tile, first, valid, x_sorted, w_gu, w_gu, b4, b4)


def _down_kernel(ex_ref, col_ref, tile_ref, first_ref, valid_ref,
                 a_ref, w_ref, b_ref, o_ref, w_s):
    s = pl.program_id(0)

    @pl.when(first_ref[s] == 1)
    def _():
        w_s[...] = w_ref[...].astype(BF16)

    @pl.when(valid_ref[s] == 1)
    def _():
        y = jnp.dot(a_ref[...], w_s[...], preferred_element_type=F32) + b_ref[...]
        for r in range(SLAB):
            o_ref[pl.ds(r, TM_DN, stride=SLAB), :] = y[:, r * LANE:(r + 1) * LANE]


def down(act, w_dn, b_dn, layer, plan):
    ex, col, tile, first, valid = plan
    grid_spec = pltpu.PrefetchScalarGridSpec(
        num_scalar_prefetch=5,
        grid=(NPAD // TM_DN,),
        in_specs=[pl.BlockSpec((TM_DN, D_FF), lambda s, ex, col, tile, first, valid: (tile[s], 0)),
                  pl.BlockSpec((None, None, D_FF, D), lambda s, ex, col, tile, first, valid: (layer, ex[s], 0, 0)),
                  pl.BlockSpec((None, None, 1, D), lambda s, ex, col, tile, first, valid: (layer, ex[s], 0, 0))],
        out_specs=pl.BlockSpec((TM_DN * SLAB, LANE), lambda s, ex, col, tile, first, valid: (tile[s], 0)),
        scratch_shapes=[pltpu.VMEM((D_FF, D), BF16)],
    )
    return pl.pallas_call(
        _down_kernel,
        grid_spec=grid_spec,
        out_shape=jax.ShapeDtypeStruct((NPAD * SLAB, LANE), F32),
        compiler_params=_params("arbitrary"),
        name="moe_down",
    )(ex, col, tile, first, valid, act, w_dn, b_dn.reshape(DEPTH, N_EXPERTS, 1, D))


def _combine_kernel(pos_ref, ys_hbm, x_ref, wt_ref, mod_ref, o_ref, buf, sem):
    i = pl.program_id(0)
    tm = TM_ROUTE

    def issue(t, carry):
        for k in range(TOP_K):
            src = pl.multiple_of(pos_ref[0, t * TOP_K + k] * SLAB, SLAB)
            dst = pl.multiple_of((k * tm + t) * SLAB, SLAB)
            pltpu.make_async_copy(ys_hbm.at[pl.ds(src, SLAB), :], buf.at[pl.ds(dst, SLAB), :], sem).start()
        return carry

    lax.fori_loop(0, tm, issue, 0)
    pltpu.make_async_copy(ys_hbm.at[pl.ds(0, TOP_K * tm * SLAB), :], buf, sem).wait()

    gate_row = mod_ref[pl.ds(_mod_row(i, tm), 1), 5 * D:6 * D]
    wk = [wt_ref[:, k:k + 1] for k in range(TOP_K)]
    for s in range(SLAB):
        acc = jnp.zeros((tm, LANE), F32)
        for k in range(TOP_K):
            acc = acc + wk[k] * buf[pl.ds(k * tm * SLAB + s, tm, stride=SLAB), :]
        cs = slice(s * LANE, (s + 1) * LANE)
        o_ref[:, cs] = x_ref[:, cs] + gate_row[:, cs] * acc


def combine(ys, pos, wts, x1, mod):
    n_steps = T // TM_ROUTE
    return pl.pallas_call(
        _combine_kernel,
        grid=(n_steps,),
        in_specs=[pl.BlockSpec((None, 1, TM_ROUTE * TOP_K), lambda i: (i, 0, 0), memory_space=pltpu.SMEM),
                  pl.BlockSpec(memory_space=pl.ANY),
                  pl.BlockSpec((TM_ROUTE, D), lambda i: (i, 0)),
                  pl.BlockSpec((TM_ROUTE, LANE), lambda i: (i, 0)),
                  pl.BlockSpec((8, N_MOD * D), lambda i: (0, 0))],
        out_specs=pl.BlockSpec((TM_ROUTE, D), lambda i: (i, 0)),
        out_shape=jax.ShapeDtypeStruct((T, D), F32),
        scratch_shapes=[pltpu.VMEM((TOP_K * TM_ROUTE * SLAB, LANE), F32), pltpu.SemaphoreType.DMA(())],
        compiler_params=_params("arbitrary"),
        name="moe_combine",
    )(pos.reshape(n_steps, 1, TM_ROUTE * TOP_K), ys, x1, wts, mod)


def kernel(x_prompt, x_sample, cache_diff_k, cache_diff_v, cache_gqa_k, cache_gqa_v, c, c_ctx, w_mod, b_mod, norm1_g, norm2_g, w_in, hy_conv_w, hy_conv_b, hy_f_w1, hy_f_b1, hy_f_w2, hy_f_b2, hy_f_w3, hy_freq, hy_skip, df_qn_g, df_kn_g, df_lq1, df_lk1, df_lq2, df_lk2, df_subln_g, gq_qn_g, gq_kn_g, w_hy_o, w_df_o, w_gq_o, w_out, router_w, router_b, w_gu, b_gu, w_dn, b_dn):
    x = jnp.concatenate([x_prompt.reshape(T_CTX, D), x_sample.reshape(T_LAT, D)], axis=0)
    cond8 = jnp.zeros((8, D), F32).at[0].set(c_ctx).at[1:1 + DEC_BATCH].set(c)
    mods = adaln_all(cond8, w_mod, b_mod)

    cdk = cache_diff_k.reshape(DEC_BATCH, DEPTH, PAST, 512)
    cdv = cache_diff_v.reshape(DEC_BATCH, DEPTH, PAST, 512)
    cgk = cache_gqa_k.reshape(DEC_BATCH, DEPTH, PAST, 256)
    cgv = cache_gqa_v.reshape(DEC_BATCH, DEPTH, PAST, 256)

    new_dk, new_dv, new_gk, new_gv = [], [], [], []
    for l in range(DEPTH):
        mod = mods[l]
        h = modulate(x, norm1_g[l], mod, 0)
        p1 = proj(h, w_in, l, 0, C_GATE, F32, None, "proj_mix")
        gates = proj(h, w_in, l, C_GATE, 3 * D, BF16, "sigmoid", "proj_gates")

        tile128 = lambda g: jnp.tile(g, LANE // g.shape[0]).reshape(1, LANE)
        gains = (tile128(df_qn_g[l]), tile128(df_kn_g[l]), tile128(gq_qn_g[l]), tile128(gq_kn_g[l]),
                 tile128(df_subln_g[l]))
        lam_init = 0.8 - 0.6 * math.exp(-0.3 * l)
        lam = (jnp.exp(jnp.sum(df_lq1[l] * df_lk1[l])) - jnp.exp(jnp.sum(df_lq2[l] * df_lk2[l])) + lam_init)

        yb_c, yc_c, ndk, ndv, ngk, ngv = attention(p1, gains, lam, l)
        yb_l, yc_l = attention(p1, gains, lam, l, caches=(cdk[:, l], cdv[:, l], cgk[:, l], cgv[:, l]))
        new_dk.append(ndk.reshape(BATCH, SEQ, DF_HEADS, 2, DF_QK))
        new_dv.append(ndv.reshape(BATCH, SEQ, DF_HEADS, DF_V))
        new_gk.append(ngk.reshape(BATCH, SEQ, GQ_KV, GQ_DIM))
        new_gv.append(ngv.reshape(BATCH, SEQ, GQ_KV, GQ_DIM))

        filt = (hy_f_w1[l], hy_f_b1[l], hy_f_w2[l], hy_f_b2[l], hy_f_w3[l], hy_freq[l])
        ya_c = hyena(p1, hy_conv_w[l], hy_conv_b[l], hy_skip[l], _hyena_filter_spectra(SEQ, *filt),
                     SEQ, BATCH, 0, 512)
        ya_l = hyena(p1, hy_conv_w[l], hy_conv_b[l], hy_skip[l], _hyena_filter_spectra(DEC_SEQ, *filt),
                     DEC_SEQ, DEC_BATCH, T_CTX // DEC_SEQ, 256)

        ya = jnp.concatenate([ya_c, ya_l], axis=0)
        yb = jnp.concatenate([yb_c, yb_l], axis=0)
        yc = jnp.concatenate([yc_c, yc_l], axis=0)
        merged = merge(ya, yb, yc, gates, w_hy_o[l].astype(BF16), w_df_o[l].astype(BF16), w_gq_o[l].astype(BF16))
        x1, h2_slab, route, wts, cnt = out_router(merged, x, w_out[l].astype(BF16), mod, norm2_g[l],
                                                  router_w[l], router_b[l])

        counts = cnt[0, :N_EXPERTS]
        seg_start = _segments(counts)[1]
        pos = seg_start[route[:, :TOP_K]] + route[:, TOP_K:2 * TOP_K]
        x_sorted = dispatch(h2_slab, pos)
        act = gate_up(x_sorted, w_gu, b_gu, l, _step_plan(counts, D_FF // TN_GU, TM_MOE))
        ys = down(act, w_dn, b_dn, l, _step_plan(counts, 1, TM_DN))
        x = combine(ys, pos, wts, x1, mod)

    y_prompt = x[:T_CTX].reshape(BATCH, SEQ, D)
    y_sample = x[T_CTX:].reshape(DEC_BATCH, DEC_SEQ, D)
    return (y_prompt, y_sample, jnp.stack(new_dk, axis=1), jnp.stack(new_dv, axis=1),
            jnp.stack(new_gk, axis=1), jnp.stack(new_gv, axis=1))
```

```python
import functools
import math

import jax
import jax.numpy as jnp
import numpy as np
from jax import lax
from jax.experimental import pallas as pl
from jax.experimental.pallas import tpu as pltpu

F32 = jnp.float32
BF16 = jnp.bfloat16

D = 2048
BATCH, SEQ = 32, 256
DEPTH = 2
DEC_BATCH, DEC_SEQ = 2, 1024
PAST = 256
GRID_W = 64
ROPE_THETA = 10000.0
EPS = 1e-6
N_MOD = 6
HY_CH = 512
HY_BANDS = 16
HY_FFN = 64
DF_HEADS, DF_QK, DF_V = 4, 64, 128
GQ_HEADS, GQ_KV, GQ_DIM = 8, 2, 128
N_EXPERTS, TOP_K = 32, 4
D_FF = D
SWIGLU_LIMIT = 7.0
SWIGLU_ALPHA = 1.702

T_CTX = BATCH * SEQ
T_LAT = DEC_BATCH * DEC_SEQ
T = T_CTX + T_LAT
N_ASSIGN = T * TOP_K

C_HY = 0
C_DQ = 3 * HY_CH
C_DK = C_DQ + 512
C_DV = C_DK + 512
C_GQ = C_DV + 512
C_GK = C_GQ + 1024
C_GV = C_GK + 256
C_GATE = C_GV + 256
D_IN = C_GATE + 3 * D

LANE = 128
VMEM_LIMIT = 56 * 1024 * 1024

TM_TOK = 512
TM_PROJ = 1024
TN_PROJ = 512
QC = 256
TM_MOE = 512
TN_GU = 512
TM_DN = 256
SLAB = 16
TM_ROUTE = 256
NT_MOE = N_ASSIGN // TM_MOE + N_EXPERTS
NPAD = NT_MOE * TM_MOE


def _params(*sem):
    return pltpu.CompilerParams(dimension_semantics=sem, vmem_limit_bytes=VMEM_LIMIT)


def _mod_row(tile_idx, tm):
    tok0 = tile_idx * tm
    return jnp.where(tok0 < T_CTX, 0, 1 + (tok0 - T_CTX) // DEC_SEQ)


def _adaln_kernel(c_ref, w_ref, b_ref, o_ref):
    c = c_ref[...]
    a = (c * jax.nn.sigmoid(c)).astype(BF16)
    o_ref[...] = jnp.dot(a, w_ref[...].astype(BF16), preferred_element_type=F32) + b_ref[...]


def adaln_all(cond8, w_mod, b_mod):
    tn = 1024
    return pl.pallas_call(
        _adaln_kernel,
        grid=(DEPTH, N_MOD * D // tn),
        in_specs=[pl.BlockSpec((8, D), lambda l, j: (0, 0)),
                  pl.BlockSpec((None, D, tn), lambda l, j: (l, 0, j)),
                  pl.BlockSpec((None, 1, tn), lambda l, j: (l, 0, j))],
        out_specs=pl.BlockSpec((None, 8, tn), lambda l, j: (l, 0, j)),
        out_shape=jax.ShapeDtypeStruct((DEPTH, 8, N_MOD * D), F32),
        compiler_params=_params("arbitrary", "arbitrary"),
        name="adaln",
    )(cond8, w_mod, b_mod.reshape(DEPTH, 1, N_MOD * D))


def _modulated_norm(x, g, sh, sc):
    ms = jnp.mean(x * x, axis=-1, keepdims=True)
    return (x * lax.rsqrt(ms + EPS) * g) * (1.0 + sc) + sh


def _modulate_kernel(x_ref, g_ref, mod_ref, o_ref, *, k_shift):
    r = _mod_row(pl.program_id(0), TM_TOK)
    sh = mod_ref[pl.ds(r, 1), k_shift * D:(k_shift + 1) * D]
    sc = mod_ref[pl.ds(r, 1), (k_shift + 1) * D:(k_shift + 2) * D]
    o_ref[...] = _modulated_norm(x_ref[...], g_ref[...], sh, sc).astype(BF16)


def modulate(x, g, mod, k_shift):
    return pl.pallas_call(
        functools.partial(_modulate_kernel, k_shift=k_shift),
        grid=(T // TM_TOK,),
        in_specs=[pl.BlockSpec((TM_TOK, D), lambda i: (i, 0)),
                  pl.BlockSpec((1, D), lambda i: (0, 0)),
                  pl.BlockSpec((8, N_MOD * D), lambda i: (0, 0))],
        out_specs=pl.BlockSpec((TM_TOK, D), lambda i: (i, 0)),
        out_shape=jax.ShapeDtypeStruct((T, D), BF16),
        compiler_params=_params("arbitrary"),
        name="modulate",
    )(x, g.reshape(1, D), mod)


def _proj_kernel(x_ref, w_ref, o_ref, wbf_ref, *, act):
    @pl.when(pl.program_id(1) == 0)
    def _():
        wbf_ref[...] = w_ref[...].astype(BF16)

    y = jnp.dot(x_ref[...], wbf_ref[...], preferred_element_type=F32)
    if act == "sigmoid":
        y = jax.nn.sigmoid(y)
    o_ref[...] = y.astype(o_ref.dtype)


def proj(h, w, layer, col0, ncols, out_dtype, act, name):
    blk0 = col0 // TN_PROJ
    return pl.pallas_call(
        functools.partial(_proj_kernel, act=act),
        grid=(ncols // TN_PROJ, T // TM_PROJ),
        in_specs=[pl.BlockSpec((TM_PROJ, D), lambda j, i: (i, 0)),
                  pl.BlockSpec((None, D, TN_PROJ), lambda j, i: (layer, 0, blk0 + j))],
        out_specs=pl.BlockSpec((TM_PROJ, TN_PROJ), lambda j, i: (i, j)),
        out_shape=jax.ShapeDtypeStruct((T, ncols), out_dtype),
        scratch_shapes=[pltpu.VMEM((D, TN_PROJ), BF16)],
        compiler_params=_params("arbitrary", "arbitrary"),
        name=name,
    )(h, w)


def _norm_halves(x, g):
    lo = lax.broadcasted_iota(jnp.int32, x.shape, 1) < DF_QK
    x2 = x * x
    s_lo = jnp.sum(jnp.where(lo, x2, 0.0), axis=-1, keepdims=True)
    s_hi = jnp.sum(jnp.where(lo, 0.0, x2), axis=-1, keepdims=True)
    inv = jnp.where(lo, lax.rsqrt(s_lo * (1.0 / DF_QK) + EPS), lax.rsqrt(s_hi * (1.0 / DF_QK) + EPS))
    return x * inv * g


def _norm_full(x, g):
    ms = jnp.mean(x * x, axis=-1, keepdims=True)
    return x * lax.rsqrt(ms + EPS) * g


def _rope(x, cos, sin_a, sin_b, shift):
    return x * cos + pltpu.roll(x, LANE - shift, 1) * sin_a + pltpu.roll(x, shift, 1) * sin_b


def _softmax(s):
    m = jnp.max(s, axis=-1, keepdims=True)
    e = jnp.exp(s - m)
    return e / jnp.sum(e, axis=-1, keepdims=True)


def _dot_nt(a, b):
    return lax.dot_general(a, b, (((1,), (1,)), ((), ())), preferred_element_type=F32)


def _attn_kernel(*refs, lq, has_ctx, out_scale):
    if has_ctx:
        (dq_ref, dk_ref, dv_ref, gq_ref, gk_ref, gv_ref, cdk_ref, cdv_ref, cgk_ref, cgv_ref,
         cd_ref, sad_ref, sbd_ref, cg_ref, sag_ref, sbg_ref, cdq_ref, sadq_ref, sbdq_ref,
         cgq_ref, sagq_ref, sbgq_ref,
         qn_ref, kn_ref, gqn_ref, gkn_ref, sub_ref, lam_ref,
         yb_ref, yc_ref, kd_s, vd_s, kg_s, vg_s) = refs
    else:
        (dq_ref, dk_ref, dv_ref, gq_ref, gk_ref, gv_ref,
         qn_ref, kn_ref, gqn_ref, gkn_ref, sub_ref, lam_ref,
         yb_ref, yc_ref, ndk_ref, ndv_ref, ngk_ref, ngv_ref, kd_s, vd_s, kg_s, vg_s) = refs

    @pl.when(pl.program_id(1) == 0)
    def _():
        for h in range(DF_HEADS):
            cs = slice(h * LANE, (h + 1) * LANE)
            kn = _norm_halves(dk_ref[:, cs], kn_ref[...])
            if has_ctx:
                kn = _rope(kn, cd_ref[...], sad_ref[...], sbd_ref[...], DF_QK // 4)
            else:
                ndk_ref[:, cs] = kn
            kd_s[0:lq, cs] = kn.astype(BF16)
        vd_s[0:lq, :] = dv_ref[...].astype(BF16)
        for g in range(GQ_KV):
            cs = slice(g * LANE, (g + 1) * LANE)
            kn = _norm_full(gk_ref[:, cs], gkn_ref[...])
            if has_ctx:
                kn = _rope(kn, cg_ref[...], sag_ref[...], sbg_ref[...], GQ_DIM // 4)
            else:
                ngk_ref[:, cs] = kn
            kg_s[0:lq, cs] = kn.astype(BF16)
        vg_s[0:lq, :] = gv_ref[...].astype(BF16)
        if has_ctx:
            kd_s[lq:lq + PAST, :] = cdk_ref[...].astype(BF16)
            vd_s[lq:lq + PAST, :] = cdv_ref[...].astype(BF16)
            kg_s[lq:lq + PAST, :] = cgk_ref[...].astype(BF16)
            vg_s[lq:lq + PAST, :] = cgv_ref[...].astype(BF16)
        else:
            ndv_ref[...] = dv_ref[...]
            ngv_ref[...] = gv_ref[...]

    lam = lam_ref[:, 0:1]
    lo = lax.broadcasted_iota(jnp.int32, (QC, LANE), 1) < DF_QK

    for h in range(DF_HEADS):
        cs = slice(h * LANE, (h + 1) * LANE)
        qn = _norm_halves(dq_ref[:, cs], qn_ref[...])
        if has_ctx:
            qn = _rope(qn, cdq_ref[...], sadq_ref[...], sbdq_ref[...], DF_QK // 4)
        qn = qn * (DF_QK ** -0.5)
        kh = kd_s[:, cs]
        p0 = _softmax(_dot_nt(jnp.where(lo, qn, 0.0).astype(BF16), kh))
        p1 = _softmax(_dot_nt(jnp.where(lo, 0.0, qn).astype(BF16), kh))
        w = (p0 - lam * p1).astype(BF16)
        o = jnp.dot(w, vd_s[:, cs], preferred_element_type=F32)
        yb_ref[:, cs] = (_norm_full(o, sub_ref[...]) * out_scale).astype(yb_ref.dtype)

    for hq in range(GQ_HEADS):
        g = hq // (GQ_HEADS // GQ_KV)
        cs = slice(hq * LANE, (hq + 1) * LANE)
        ks = slice(g * LANE, (g + 1) * LANE)
        qn = _norm_full(gq_ref[:, cs], gqn_ref[...])
        if has_ctx:
            qn = _rope(qn, cgq_ref[...], sagq_ref[...], sbgq_ref[...], GQ_DIM // 4)
        s = _dot_nt(qn.astype(BF16), kg_s[:, ks]) * (GQ_DIM ** -0.5)
        p = _softmax(s).astype(BF16)
        yc_ref[:, cs] = jnp.dot(p, vg_s[:, ks], preferred_element_type=F32).astype(yc_ref.dtype)


def _rope_tables(d_head):
    t = np.arange(DEC_SEQ)
    row, col = (t // GRID_W).astype(np.float64), (t % GRID_W).astype(np.float64)
    q = d_head // 4
    inv = ROPE_THETA ** (-np.arange(q, dtype=np.float64) / q)
    ar, ac = row[:, None] * inv, col[:, None] * inv
    z = np.zeros_like(ar)
    cos = np.concatenate([np.cos(ar), np.cos(ar), np.cos(ac), np.cos(ac)], -1)
    sin_a = np.concatenate([-np.sin(ar), z, -np.sin(ac), z], -1)
    sin_b = np.concatenate([z, np.sin(ar), z, np.sin(ac)], -1)
    rep = LANE // d_head
    return tuple(jnp.asarray(np.tile(a, (1, rep)), F32) for a in (cos, sin_a, sin_b))


def attention(p1, gains, lam, layer_idx, caches=None):
    has_ctx = caches is not None
    lq = DEC_SEQ if has_ctx else SEQ
    n_seq = DEC_BATCH if has_ctx else BATCH
    row0 = (T_CTX // lq) if has_ctx else 0
    lk = lq + (PAST if has_ctx else 0)
    nq = lq // QC
    lam_init = 0.8 - 0.6 * math.exp(-0.3 * layer_idx)

    def qspec(width, col0):
        return pl.BlockSpec((QC, width), lambda b, q: ((row0 + b) * nq + q, col0 // width))

    def kspec(width, col0):
        return pl.BlockSpec((lq, width), lambda b, q: (row0 + b, col0 // width))

    const = lambda shape: pl.BlockSpec(shape, lambda b, q: (0,) * len(shape))
    in_specs = [qspec(512, C_DQ), kspec(512, C_DK), kspec(512, C_DV),
                qspec(1024, C_GQ), kspec(256, C_GK), kspec(256, C_GV)]
    args = [p1] * 6
    if has_ctx:
        cdk, cdv, cgk, cgv = caches
        in_specs += [pl.BlockSpec((None, PAST, 512), lambda b, q: (b, 0, 0)),
                     pl.BlockSpec((None, PAST, 512), lambda b, q: (b, 0, 0)),
                     pl.BlockSpec((None, PAST, 256), lambda b, q: (b, 0, 0)),
                     pl.BlockSpec((None, PAST, 256), lambda b, q: (b, 0, 0))]
        args += [cdk, cdv, cgk, cgv]
        td, tg = _rope_tables(DF_QK), _rope_tables(GQ_DIM)
        in_specs += [const((lq, LANE))] * 6 + [pl.BlockSpec((QC, LANE), lambda b, q: (q, 0))] * 6
        args += list(td) + list(tg) + list(td) + list(tg)
    in_specs += [const((1, LANE))] * 6
    args += list(gains) + [jnp.full((1, LANE), lam, F32)]

    out_specs = [pl.BlockSpec((QC, 512), lambda b, q: ((row0 + b) * nq + q - row0 * nq, 0)),
                 pl.BlockSpec((QC, 1024), lambda b, q: ((row0 + b) * nq + q - row0 * nq, 0))]
    n_rows = n_seq * lq
    out_shape = [jax.ShapeDtypeStruct((n_rows, 512), BF16), jax.ShapeDtypeStruct((n_rows, 1024), BF16)]
    if not has_ctx:
        out_specs += [pl.BlockSpec((lq, 512), lambda b, q: (b, 0)), pl.BlockSpec((lq, 512), lambda b, q: (b, 0)),
                      pl.BlockSpec((lq, 256), lambda b, q: (b, 0)), pl.BlockSpec((lq, 256), lambda b, q: (b, 0))]
        out_shape += [jax.ShapeDtypeStruct((n_rows, 512), F32), jax.ShapeDtypeStruct((n_rows, 512), F32),
                      jax.ShapeDtypeStruct((n_rows, 256), F32), jax.ShapeDtypeStruct((n_rows, 256), F32)]
    return pl.pallas_call(
        functools.partial(_attn_kernel, lq=lq, has_ctx=has_ctx, out_scale=1.0 - lam_init),
        grid=(n_seq, nq),
        in_specs=in_specs,
        out_specs=out_specs,
        out_shape=out_shape,
        scratch_shapes=[pltpu.VMEM((lk, 512), BF16), pltpu.VMEM((lk, 512), BF16),
                        pltpu.VMEM((lk, 256), BF16), pltpu.VMEM((lk, 256), BF16)],
        compiler_params=_params("arbitrary", "arbitrary"),
        name="attn_latent" if has_ctx else "attn_context",
    )(*args)


def _short_conv(u, w_ref, b_ref, length):
    row = lax.broadcasted_iota(jnp.int32, u.shape, 0)
    prev = jnp.where(row == 0, 0.0, pltpu.roll(u, 1, 0))
    nxt = jnp.where(row == length - 1, 0.0, pltpu.roll(u, length - 1, 0))
    return w_ref[0:1, :] * prev + w_ref[1:2, :] * u + w_ref[2:3, :] * nxt + b_ref[...]


def _hyena_kernel(v_ref, x_ref, cwv_ref, cbv_ref, cwx_ref, cbx_ref, skip_ref, fm_ref, gm_ref,
                  a_ref, b_ref, a2_ref, o_ref, z_s, *, length):
    n = pl.program_id(2)

    @pl.when(n == 0)
    def _():
        z_s[...] = _short_conv(v_ref[...], cwv_ref, cbv_ref, length)

    z = z_s[...]
    xn = _short_conv(x_ref[...], cwx_ref, cbx_ref, length)
    u = jnp.dot(fm_ref[...], z.astype(BF16), preferred_element_type=F32)
    ur, ui = u[:length], u[length:]
    a, b, a2 = a_ref[...], b_ref[...], a2_ref[...]
    yr = (ur * a - ui * b).astype(BF16)
    yi = (ur * b + ui * a2).astype(BF16)
    conv = (jnp.dot(gm_ref[:, :length], yr, preferred_element_type=F32)
            + jnp.dot(gm_ref[:, length:], yi, preferred_element_type=F32))
    z = xn * (conv + skip_ref[...] * z)
    z_s[...] = z
    o_ref[...] = z.astype(o_ref.dtype)


def _dft_matrices(length):
    n = 2 * length
    f = np.arange(length)[:, None]
    t = np.arange(length)[None, :]
    ang = ((f * t) % n).astype(np.float64) * (2.0 * math.pi / n)
    cos, sin = np.cos(ang), np.sin(ang)
    alt = np.broadcast_to(np.where(t % 2 == 0, 1.0, -1.0), sin.shape)
    fm = np.concatenate([cos, np.where(f == 0, alt, -sin)], axis=0)
    wre = np.where(f == 0, 1.0 / n, 2.0 / n) * cos
    wim = np.where(f == 0, alt / n, -(2.0 / n) * sin)
    gm = np.concatenate([wre, wim], axis=0).T
    return jnp.asarray(fm, BF16), jnp.asarray(gm, BF16)


def _hyena_filter_spectra(length, w1, b1, w2, b2, w3, freq):
    hp = lax.Precision.HIGHEST
    t = jnp.linspace(0.0, 1.0, length, dtype=F32)[:, None]
    bands = jnp.linspace(1e-4, HY_BANDS - 1, HY_BANDS, dtype=F32)
    wpos = (2 * math.pi / length) * jnp.arange(length, dtype=F32)[:, None] * bands
    z = jnp.concatenate([t, jnp.cos(wpos), -jnp.sin(wpos)], -1)
    h = jnp.sin(freq * (jnp.dot(z, w1, precision=hp) + b1))
    h = jnp.sin(freq * (jnp.dot(h, w2, precision=hp) + b2))
    h = jnp.dot(h, w3, precision=hp).reshape(length, 2, 2, HY_CH)
    max_decay = math.log(1e-2) / 0.3
    min_decay = math.log(1e-2) / 1.5
    deltas = jnp.linspace(min_decay, max_decay, HY_CH, dtype=F32)
    h = h * jnp.exp(-t[:, :, None, None] * jnp.abs(deltas))
    fwd, bwd = h[:, :, 0], h[:, :, 1]
    k = jnp.concatenate([fwd, jnp.zeros((1, 2, HY_CH), F32), jnp.flip(bwd[1:], 0)], 0)
    kf = jnp.fft.rfft(k, axis=0)
    a = jnp.moveaxis(jnp.real(kf[:length]), 1, 0)
    b = jnp.moveaxis(jnp.imag(kf[:length]), 1, 0).at[:, 0].set(0.0)
    a2 = a.at[:, 0].set(jnp.real(kf[length]))
    return a, b, a2


def hyena(p1, conv_w, conv_b, skip, spectra, length, n_seq, row0, cw):
    fm, gm = _dft_matrices(length)
    a, b, a2 = spectra
    ncb = HY_CH // cw
    vblk = 2 * ncb
    const = lambda shape: pl.BlockSpec(shape, lambda s, c, n: (0,) * len(shape))
    spec_spec = pl.BlockSpec((None, length, cw), lambda s, c, n: (n, 0, c))
    return pl.pallas_call(
        functools.partial(_hyena_kernel, length=length),
        grid=(n_seq, ncb, 2),
        in_specs=[pl.BlockSpec((length, cw), lambda s, c, n: (row0 + s, vblk + c)),
                  pl.BlockSpec((length, cw), lambda s, c, n: (row0 + s, n * ncb + c)),
                  pl.BlockSpec((3, cw), lambda s, c, n: (0, vblk + c)),
                  pl.BlockSpec((1, cw), lambda s, c, n: (0, vblk + c)),
                  pl.BlockSpec((3, cw), lambda s, c, n: (0, n * ncb + c)),
                  pl.BlockSpec((1, cw), lambda s, c, n: (0, n * ncb + c)),
                  pl.BlockSpec((None, 1, cw), lambda s, c, n: (n, 0, c)),
                  const((2 * length, length)), const((length, 2 * length)),
                  spec_spec, spec_spec, spec_spec],
        out_specs=pl.BlockSpec((length, cw), lambda s, c, n: (s, c)),
        out_shape=jax.ShapeDtypeStruct((n_seq * length, HY_CH), BF16),
        scratch_shapes=[pltpu.VMEM((length, cw), F32)],
        compiler_params=_params("arbitrary", "arbitrary", "arbitrary"),
        name=f"hyena_{length}",
    )(p1, p1, conv_w, conv_b.reshape(1, -1), conv_w, conv_b.reshape(1, -1),
      skip.reshape(2, 1, HY_CH), fm, gm, a, b, a2)


def _merge_kernel(ya_ref, yb_ref, yc_ref, g_ref, wa_ref, wb_ref, wc_ref, o_ref):
    a = jnp.dot(ya_ref[...], wa_ref[...], preferred_element_type=F32)
    b = jnp.dot(yb_ref[...], wb_ref[...], preferred_element_type=F32)
    c = jnp.dot(yc_ref[...], wc_ref[...], preferred_element_type=F32)
    m = g_ref[:, 0:D] * a + g_ref[:, D:2 * D] * b + g_ref[:, 2 * D:3 * D] * c
    o_ref[...] = m.astype(o_ref.dtype)


def merge(ya, yb, yc, gates, wa, wb, wc):
    tm = 256
    const = lambda shape: pl.BlockSpec(shape, lambda i: (0, 0))
    return pl.pallas_call(
        _merge_kernel,
        grid=(T // tm,),
        in_specs=[pl.BlockSpec((tm, 512), lambda i: (i, 0)), pl.BlockSpec((tm, 512), lambda i: (i, 0)),
                  pl.BlockSpec((tm, 1024), lambda i: (i, 0)), pl.BlockSpec((tm, 3 * D), lambda i: (i, 0)),
                  const((512, D)), const((512, D)), const((1024, D))],
        out_specs=pl.BlockSpec((tm, D), lambda i: (i, 0)),
        out_shape=jax.ShapeDtypeStruct((T, D), BF16),
        compiler_params=_params("arbitrary"),
        name="merge",
    )(ya, yb, yc, gates, wa, wb, wc)


def _split3(x):
    hi = x.astype(BF16)
    return hi, (x - hi.astype(F32)).astype(BF16)


def _out_router_kernel(m_ref, x_ref, wout_ref, mod_ref, g_ref, rw_ref, rb_ref,
                       x1_ref, h2_ref, route_ref, wt_ref, cnt_ref, cnt_s, *, tm):
    r = _mod_row(pl.program_id(0), tm)
    o = jnp.dot(m_ref[...], wout_ref[...], preferred_element_type=F32)
    x1 = x_ref[...] + mod_ref[pl.ds(r, 1), 2 * D:3 * D] * o
    x1_ref[...] = x1
    h2 = _modulated_norm(x1, g_ref[...], mod_ref[pl.ds(r, 1), 3 * D:4 * D], mod_ref[pl.ds(r, 1), 4 * D:5 * D])
    for s in range(SLAB):
        h2_ref[pl.ds(s, tm, stride=SLAB), :] = h2[:, s * LANE:(s + 1) * LANE]

    h_hi, h_lo = _split3(h2)
    w_hi, w_lo = _split3(rw_ref[...])
    logits = (jnp.dot(h_hi, w_hi, preferred_element_type=F32) + jnp.dot(h_hi, w_lo, preferred_element_type=F32)
              + jnp.dot(h_lo, w_hi, preferred_element_type=F32)) + rb_ref[...]

    lane = lax.broadcasted_iota(jnp.int32, logits.shape, 1)
    vals, idxs = [], []
    for _ in range(TOP_K):
        mx = jnp.max(logits, axis=-1, keepdims=True)
        ix = jnp.min(jnp.where(logits == mx, lane, LANE), axis=-1, keepdims=True)
        vals.append(mx)
        idxs.append(ix)
        logits = jnp.where(lane == ix, -jnp.inf, logits)
    es = [jnp.exp(v - vals[0]) for v in vals]
    inv = 1.0 / (es[0] + es[1] + es[2] + es[3])

    @pl.when(pl.program_id(0) == 0)
    def _():
        cnt_s[...] = jnp.zeros_like(cnt_s)

    chosen = jnp.zeros(lane.shape, F32)
    for k in range(TOP_K):
        chosen = jnp.where(lane == idxs[k], 1.0, chosen)
    earlier = (lax.broadcasted_iota(jnp.int32, (tm, tm), 0) > lax.broadcasted_iota(jnp.int32, (tm, tm), 1))
    before = jnp.dot(earlier.astype(BF16), chosen.astype(BF16), preferred_element_type=F32) + cnt_s[...]
    route_out = jnp.zeros(lane.shape, jnp.int32)
    wt_out = jnp.zeros(lane.shape, F32)
    for k in range(TOP_K):
        rank = jnp.sum(jnp.where(lane == idxs[k], before, 0.0), axis=-1, keepdims=True).astype(jnp.int32)
        route_out = jnp.where(lane == k, idxs[k], route_out)
        route_out = jnp.where(lane == TOP_K + k, rank, route_out)
        wt_out = jnp.where(lane == k, es[k] * inv, wt_out)
    route_ref[...] = route_out
    wt_ref[...] = wt_out
    total = cnt_s[...] + jnp.sum(chosen, axis=0, keepdims=True)
    cnt_s[...] = total
    cnt_ref[...] = total.astype(jnp.int32)


def out_router(merged, x, w_out, mod, norm2_g, router_w, router_b):
    tm = TM_ROUTE
    const = lambda shape: pl.BlockSpec(shape, lambda i: (0, 0))
    rw = jnp.zeros((D, LANE), F32).at[:, :N_EXPERTS].set(router_w)
    rb = jnp.full((1, LANE), -1e30, F32).at[0, :N_EXPERTS].set(router_b)
    return pl.pallas_call(
        functools.partial(_out_router_kernel, tm=tm),
        grid=(T // tm,),
        in_specs=[pl.BlockSpec((tm, D), lambda i: (i, 0)), pl.BlockSpec((tm, D), lambda i: (i, 0)),
                  const((D, D)), const((8, N_MOD * D)), const((1, D)), const((D, LANE)), const((1, LANE))],
        out_specs=[pl.BlockSpec((tm, D), lambda i: (i, 0)), pl.BlockSpec((tm * SLAB, LANE), lambda i: (i, 0)),
                   pl.BlockSpec((tm, LANE), lambda i: (i, 0)), pl.BlockSpec((tm, LANE), lambda i: (i, 0)),
                   const((1, LANE))],
        out_shape=[jax.ShapeDtypeStruct((T, D), F32), jax.ShapeDtypeStruct((T * SLAB, LANE), F32),
                   jax.ShapeDtypeStruct((T, LANE), jnp.int32), jax.ShapeDtypeStruct((T, LANE), F32),
                   jax.ShapeDtypeStruct((1, LANE), jnp.int32)],
        scratch_shapes=[pltpu.VMEM((1, LANE), F32)],
        compiler_params=_params("arbitrary"),
        name="out_router",
    )(merged, x, w_out, mod, norm2_g.reshape(1, D), rw, rb)


def _segments(counts):
    padded = ((counts + TM_MOE - 1) // TM_MOE) * TM_MOE
    seg_end = jnp.cumsum(padded)
    return padded, seg_end - padded, seg_end


def _step_plan(counts, n_col_blocks, tm):
    padded, seg_start, seg_end = _segments(counts)
    n_tiles = seg_end[-1] // tm
    step = jnp.arange((NPAD // tm) * n_col_blocks, dtype=jnp.int32)
    tile_probe = jnp.minimum(step // n_col_blocks, n_tiles - 1)
    ends_before = (seg_end[None, :] <= (tile_probe * tm)[:, None]).astype(jnp.int32)
    ex = jnp.minimum(jnp.sum(ends_before, axis=1), N_EXPERTS - 1)
    onehot = (ex[:, None] == jnp.arange(N_EXPERTS, dtype=jnp.int32)[None, :]).astype(jnp.int32)
    t0 = jnp.sum(onehot * seg_start[None, :], axis=1) // tm
    ne = jnp.maximum(jnp.sum(onehot * padded[None, :], axis=1) // tm, 1)
    valid = step < n_tiles * n_col_blocks
    local = jnp.where(valid, step - n_col_blocks * t0, n_col_blocks * ne - 1)
    col = local // ne
    tile = t0 + local % ne
    first = jnp.logical_and(valid, local % ne == 0)
    rows = jnp.clip(jnp.sum(onehot * counts[None, :], axis=1) - (tile - t0) * tm, 0, tm)
    last_col = col == n_col_blocks - 1
    ends_before_next = (seg_end[None, :] <= ((t0 + ne) * tm)[:, None]).astype(jnp.int32)
    next_ex = jnp.where(last_col, jnp.minimum(jnp.sum(ends_before_next, axis=1), N_EXPERTS - 1), ex)
    next_col = jnp.where(last_col, 0, col + 1)
    next_ok = jnp.logical_and(first, jnp.logical_or(~last_col, t0 + ne < n_tiles))
    lower = (jnp.arange(N_EXPERTS, dtype=jnp.int32)[None, :] < ex[:, None]).astype(jnp.int32)
    blocks_before = n_col_blocks * jnp.sum(lower * (padded > 0).astype(jnp.int32)[None, :], axis=1) + col
    i32 = lambda a: a.astype(jnp.int32)
    return dict(ex=i32(ex), col=i32(col), tile=i32(tile), first=i32(first), valid=i32(valid), rows=i32(rows),
                next_ex=i32(next_ex), next_col=i32(next_col), next_ok=i32(next_ok), slot=i32(blocks_before % 2))


def _dispatch_kernel(pos_ref, h_ref, xs_hbm, sem):
    def issue(t, carry):
        src = pl.multiple_of(t * SLAB, SLAB)
        for k in range(TOP_K):
            dst = pl.multiple_of(pos_ref[0, t * TOP_K + k] * SLAB, SLAB)
            pltpu.make_async_copy(h_ref.at[pl.ds(src, SLAB), :], xs_hbm.at[pl.ds(dst, SLAB), :],
                                  sem).start(priority=k % 2)
        return carry

    lax.fori_loop(0, TM_ROUTE, issue, 0)
    for _ in range(TOP_K):
        pltpu.make_async_copy(h_ref, xs_hbm.at[pl.ds(0, TM_ROUTE * SLAB), :], sem).wait()


def dispatch(h2_slab, pos):
    n_steps = T // TM_ROUTE
    return pl.pallas_call(
        _dispatch_kernel,
        grid=(n_steps,),
        in_specs=[pl.BlockSpec((None, 1, TM_ROUTE * TOP_K), lambda i: (i, 0, 0), memory_space=pltpu.SMEM),
                  pl.BlockSpec((TM_ROUTE * SLAB, LANE), lambda i: (i, 0))],
        out_specs=pl.BlockSpec(memory_space=pl.ANY),
        out_shape=jax.ShapeDtypeStruct((NPAD * SLAB, LANE), F32),
        scratch_shapes=[pltpu.SemaphoreType.DMA(())],
        compiler_params=_params("arbitrary"),
        name="moe_dispatch",
    )(pos.reshape(n_steps, 1, TM_ROUTE * TOP_K), h2_slab)


_PLAN_KEYS = ("ex", "col", "tile", "first", "valid", "rows", "next_ex", "next_col", "next_ok", "slot")


def _stream_weights(s, plan, fetch, on_arrival):
    @pl.when(s == 0)
    def _():
        for cp in fetch(plan["ex"][0], plan["col"][0], 0):
            cp.start()

    @pl.when(plan["first"][s] == 1)
    def _():
        slot = plan["slot"][s]
        for cp in fetch(plan["ex"][s], plan["col"][s], slot):
            cp.wait()

        @pl.when(plan["next_ok"][s] == 1)
        def _():
            for cp in fetch(plan["next_ex"][s], plan["next_col"][s], 1 - slot):
                cp.start()

        on_arrival(slot)


def _gate_up_kernel(*refs, layer):
    plan = dict(zip(_PLAN_KEYS, refs[:len(_PLAN_KEYS)]))
    x_ref, w_hbm, bg_ref, bu_ref, o_ref, stage, wg_s, wu_s, sem = refs[len(_PLAN_KEYS):]
    s = pl.program_id(0)

    def fetch(e, c, slot):
        c0 = pl.multiple_of(c * TN_GU, TN_GU)
        return [pltpu.make_async_copy(w_hbm.at[layer, e, :, pl.ds(c0, TN_GU)], stage.at[slot, 0], sem.at[slot]),
                pltpu.make_async_copy(w_hbm.at[layer, e, :, pl.ds(D_FF + c0, TN_GU)], stage.at[slot, 1], sem.at[slot])]

    def on_arrival(slot):
        wg_s[...] = stage[slot, 0].astype(BF16)
        wu_s[...] = stage[slot, 1].astype(BF16)

    _stream_weights(s, plan, fetch, on_arrival)

    def compute(rows):
        g = jnp.broadcast_to(bg_ref[...], (rows, TN_GU))
        u = jnp.broadcast_to(bu_ref[...], (rows, TN_GU))
        for c in range(SLAB // 2):
            xc = jnp.concatenate([x_ref[pl.ds(2 * c, rows, stride=SLAB), :],
                                  x_ref[pl.ds(2 * c + 1, rows, stride=SLAB), :]], axis=1).astype(BF16)
            ks = slice(c * 2 * LANE, (c + 1) * 2 * LANE)
            g = g + jnp.dot(xc, wg_s[ks, :], preferred_element_type=F32)
            u = u + jnp.dot(xc, wu_s[ks, :], preferred_element_type=F32)
        gate = jnp.minimum(g, SWIGLU_LIMIT)
        up = jnp.clip(u, -SWIGLU_LIMIT, SWIGLU_LIMIT)
        o_ref[0:rows, :] = (gate * jax.nn.sigmoid(SWIGLU_ALPHA * gate) * (up + 1.0)).astype(o_ref.dtype)

    live = plan["valid"][s] == 1
    half = plan["rows"][s] <= TM_MOE // 2

    @pl.when(jnp.logical_and(live, jnp.logical_not(half)))
    def _():
        compute(TM_MOE)

    @pl.when(jnp.logical_and(live, half))
    def _():
        compute(TM_MOE // 2)


def gate_up(x_sorted, w_gu, b_gu, layer, plan):
    ncb = D_FF // TN_GU
    n = len(_PLAN_KEYS)
    imap = lambda f: (lambda s, *p: f(s, dict(zip(_PLAN_KEYS, p))))
    grid_spec = pltpu.PrefetchScalarGridSpec(
        num_scalar_prefetch=n,
        grid=(NT_MOE * ncb,),
        in_specs=[pl.BlockSpec((TM_MOE * SLAB, LANE), imap(lambda s, p: (p["tile"][s], 0))),
                  pl.BlockSpec(memory_space=pl.ANY),
                  pl.BlockSpec((None, None, 1, TN_GU), imap(lambda s, p: (layer, p["ex"][s], 0, p["col"][s]))),
                  pl.BlockSpec((None, None, 1, TN_GU), imap(lambda s, p: (layer, p["ex"][s], 0, ncb + p["col"][s])))],
        out_specs=pl.BlockSpec((TM_MOE, TN_GU), imap(lambda s, p: (p["tile"][s], p["col"][s]))),
        scratch_shapes=[pltpu.VMEM((2, 2, D, TN_GU), F32), pltpu.VMEM((D, TN_GU), BF16), pltpu.VMEM((D, TN_GU), BF16),
                        pltpu.SemaphoreType.DMA((2,))],
    )
    b4 = b_gu.reshape(DEPTH, N_EXPERTS, 1, 2 * D_FF)
    return pl.pallas_call(
        functools.partial(_gate_up_kernel, layer=layer),
        grid_spec=grid_spec,
        out_shape=jax.ShapeDtypeStruct((NPAD, D_FF), BF16),
        compiler_params=_params("arbitrary"),
        name="moe_gate_up",
    )(*[plan[k] for k in _PLAN_KEYS], x_sorted, w_gu, b4, b4)


def _down_kernel(*refs, layer):
    plan = dict(zip(_PLAN_KEYS, refs[:len(_PLAN_KEYS)]))
    a_ref, w_hbm, b_ref, o_ref, stage, w_s, sem = refs[len(_PLAN_KEYS):]
    s = pl.program_id(0)

    def fetch(e, c, slot):
        return [pltpu.make_async_copy(w_hbm.at[layer, e], stage.at[slot], sem.at[slot])]

    def on_arrival(slot):
        w_s[...] = stage[slot].astype(BF16)

    _stream_weights(s, plan, fetch, on_arrival)

    @pl.when(jnp.logical_and(plan["valid"][s] == 1, plan["rows"][s] > 0))
    def _():
        y = jnp.dot(a_ref[...], w_s[...], preferred_element_type=F32) + b_ref[...]
        for r in range(SLAB):
            o_ref[pl.ds(r, TM_DN, stride=SLAB), :] = y[:, r * LANE:(r + 1) * LANE]


def down(act, w_dn, b_dn, layer, plan):
    n = len(_PLAN_KEYS)
    imap = lambda f: (lambda s, *p: f(s, dict(zip(_PLAN_KEYS, p))))
    grid_spec = pltpu.PrefetchScalarGridSpec(
        num_scalar_prefetch=n,
        grid=(NPAD // TM_DN,),
        in_specs=[pl.BlockSpec((TM_DN, D_FF), imap(lambda s, p: (p["tile"][s], 0))),
                  pl.BlockSpec(memory_space=pl.ANY),
                  pl.BlockSpec((None, None, 1, D), imap(lambda s, p: (layer, p["ex"][s], 0, 0)))],
        out_specs=pl.BlockSpec((TM_DN * SLAB, LANE), imap(lambda s, p: (p["tile"][s], 0))),
        scratch_shapes=[pltpu.VMEM((2, D_FF, D), F32), pltpu.VMEM((D_FF, D), BF16), pltpu.SemaphoreType.DMA((2,))],
    )
    return pl.pallas_call(
        functools.partial(_down_kernel, layer=layer),
        grid_spec=grid_spec,
        out_shape=jax.ShapeDtypeStruct((NPAD * SLAB, LANE), F32),
        compiler_params=_params("arbitrary"),
        name="moe_down",
    )(*[plan[k] for k in _PLAN_KEYS], act, w_dn, b_dn.reshape(DEPTH, N_EXPERTS, 1, D))


def _combine_kernel(pos_ref, ys_hbm, x_ref, wt_ref, mod_ref, o_ref, buf, sem):
    i = pl.program_id(0)
    tm = TM_ROUTE

    def issue(t, carry):
        for k in range(TOP_K):
            src = pl.multiple_of(pos_ref[0, t * TOP_K + k] * SLAB, SLAB)
            dst = pl.multiple_of((k * tm + t) * SLAB, SLAB)
            pltpu.make_async_copy(ys_hbm.at[pl.ds(src, SLAB), :], buf.at[pl.ds(dst, SLAB), :],
                                  sem).start(priority=k % 2)
        return carry

    lax.fori_loop(0, tm, issue, 0)
    pltpu.make_async_copy(ys_hbm.at[pl.ds(0, TOP_K * tm * SLAB), :], buf, sem).wait()

    gate_row = mod_ref[pl.ds(_mod_row(i, tm), 1), 5 * D:6 * D]
    wk = [wt_ref[:, k:k + 1] for k in range(TOP_K)]
    for s in range(SLAB):
        acc = jnp.zeros((tm, LANE), F32)
        for k in range(TOP_K):
            acc = acc + wk[k] * buf[pl.ds(k * tm * SLAB + s, tm, stride=SLAB), :]
        cs = slice(s * LANE, (s + 1) * LANE)
        o_ref[:, cs] = x_ref[:, cs] + gate_row[:, cs] * acc


def combine(ys, pos, wts, x1, mod):
    n_steps = T // TM_ROUTE
    return pl.pallas_call(
        _combine_kernel,
        grid=(n_steps,),
        in_specs=[pl.BlockSpec((None, 1, TM_ROUTE * TOP_K), lambda i: (i, 0, 0), memory_space=pltpu.SMEM),
                  pl.BlockSpec(memory_space=pl.ANY),
                  pl.BlockSpec((TM_ROUTE, D), lambda i: (i, 0)),
                  pl.BlockSpec((TM_ROUTE, LANE), lambda i: (i, 0)),
                  pl.BlockSpec((8, N_MOD * D), lambda i: (0, 0))],
        out_specs=pl.BlockSpec((TM_ROUTE, D), lambda i: (i, 0)),
        out_shape=jax.ShapeDtypeStruct((T, D), F32),
        scratch_shapes=[pltpu.VMEM((TOP_K * TM_ROUTE * SLAB, LANE), F32), pltpu.SemaphoreType.DMA(())],
        compiler_params=_params("arbitrary"),
        name="moe_combine",
    )(pos.reshape(n_steps, 1, TM_ROUTE * TOP_K), ys, x1, wts, mod)


def kernel(x_prompt, x_sample, cache_diff_k, cache_diff_v, cache_gqa_k, cache_gqa_v, c, c_ctx, w_mod, b_mod, norm1_g, norm2_g, w_in, hy_conv_w, hy_conv_b, hy_f_w1, hy_f_b1, hy_f_w2, hy_f_b2, hy_f_w3, hy_freq, hy_skip, df_qn_g, df_kn_g, df_lq1, df_lk1, df_lq2, df_lk2, df_subln_g, gq_qn_g, gq_kn_g, w_hy_o, w_df_o, w_gq_o, w_out, router_w, router_b, w_gu, b_gu, w_dn, b_dn):
    x = jnp.concatenate([x_prompt.reshape(T_CTX, D), x_sample.reshape(T_LAT, D)], axis=0)
    cond8 = jnp.zeros((8, D), F32).at[0].set(c_ctx).at[1:1 + DEC_BATCH].set(c)
    mods = adaln_all(cond8, w_mod, b_mod)

    cdk = cache_diff_k.reshape(DEC_BATCH, DEPTH, PAST, 512)
    cdv = cache_diff_v.reshape(DEC_BATCH, DEPTH, PAST, 512)
    cgk = cache_gqa_k.reshape(DEC_BATCH, DEPTH, PAST, 256)
    cgv = cache_gqa_v.reshape(DEC_BATCH, DEPTH, PAST, 256)

    new_dk, new_dv, new_gk, new_gv = [], [], [], []
    for l in range(DEPTH):
        mod = mods[l]
        h = modulate(x, norm1_g[l], mod, 0)
        p1 = proj(h, w_in, l, 0, C_GATE, F32, None, "proj_mix")
        gates = proj(h, w_in, l, C_GATE, 3 * D, BF16, "sigmoid", "proj_gates")

        tile128 = lambda g: jnp.tile(g, LANE // g.shape[0]).reshape(1, LANE)
        gains = (tile128(df_qn_g[l]), tile128(df_kn_g[l]), tile128(gq_qn_g[l]), tile128(gq_kn_g[l]),
                 tile128(df_subln_g[l]))
        lam_init = 0.8 - 0.6 * math.exp(-0.3 * l)
        lam = (jnp.exp(jnp.sum(df_lq1[l] * df_lk1[l])) - jnp.exp(jnp.sum(df_lq2[l] * df_lk2[l])) + lam_init)

        yb_c, yc_c, ndk, ndv, ngk, ngv = attention(p1, gains, lam, l)
        yb_l, yc_l = attention(p1, gains, lam, l, caches=(cdk[:, l], cdv[:, l], cgk[:, l], cgv[:, l]))
        new_dk.append(ndk.reshape(BATCH, SEQ, DF_HEADS, 2, DF_QK))
        new_dv.append(ndv.reshape(BATCH, SEQ, DF_HEADS, DF_V))
        new_gk.append(ngk.reshape(BATCH, SEQ, GQ_KV, GQ_DIM))
        new_gv.append(ngv.reshape(BATCH, SEQ, GQ_KV, GQ_DIM))

        filt = (hy_f_w1[l], hy_f_b1[l], hy_f_w2[l], hy_f_b2[l], hy_f_w3[l], hy_freq[l])
        ya_c = hyena(p1, hy_conv_w[l], hy_conv_b[l], hy_skip[l], _hyena_filter_spectra(SEQ, *filt),
                     SEQ, BATCH, 0, 512)
        ya_l = hyena(p1, hy_conv_w[l], hy_conv_b[l], hy_skip[l], _hyena_filter_spectra(DEC_SEQ, *filt),
                     DEC_SEQ, DEC_BATCH, T_CTX // DEC_SEQ, 256)

        ya = jnp.concatenate([ya_c, ya_l], axis=0)
        yb = jnp.concatenate([yb_c, yb_l], axis=0)
        yc = jnp.concatenate([yc_c, yc_l], axis=0)
        merged = merge(ya, yb, yc, gates, w_hy_o[l].astype(BF16), w_df_o[l].astype(BF16), w_gq_o[l].astype(BF16))
        x1, h2_slab, route, wts, cnt = out_router(merged, x, w_out[l].astype(BF16), mod, norm2_g[l],
                                                  router_w[l], router_b[l])

        counts = cnt[0, :N_EXPERTS]
        seg_start = _segments(counts)[1]
        pos = seg_start[route[:, :TOP_K]] + route[:, TOP_K:2 * TOP_K]
        x_sorted = dispatch(h2_slab, pos)
        act = gate_up(x_sorted, w_gu, b_gu, l, _step_plan(counts, D_FF // TN_GU, TM_MOE))
        ys = down(act, w_dn, b_dn, l, _step_plan(counts, 1, TM_DN))
        x = combine(ys, pos, wts, x1, mod)

    y_prompt = x[:T_CTX].reshape(BATCH, SEQ, D)
    y_sample = x[T_CTX:].reshape(DEC_BATCH, DEC_SEQ, D)
    return (y_prompt, y_sample, jnp.stack(new_dk, axis=1), jnp.stack(new_dv, axis=1),
            jnp.stack(new_gk, axis=1), jnp.stack(new_gv, axis=1))
```

```python
import functools
import math

import jax
import jax.numpy as jnp
import numpy as np
from jax import lax
from jax.experimental import pallas as pl
from jax.experimental.pallas import tpu as pltpu

F32 = jnp.float32
BF16 = jnp.bfloat16

D = 2048
BATCH, SEQ = 32, 256
DEPTH = 2
DEC_BATCH, DEC_SEQ = 2, 1024
PAST = 256
GRID_W = 64
ROPE_THETA = 10000.0
EPS = 1e-6
N_MOD = 6
HY_CH = 512
HY_BANDS = 16
HY_FFN = 64
DF_HEADS, DF_QK, DF_V = 4, 64, 128
GQ_HEADS, GQ_KV, GQ_DIM = 8, 2, 128
N_EXPERTS, TOP_K = 32, 4
D_FF = D
SWIGLU_LIMIT = 7.0
SWIGLU_ALPHA = 1.702

T_CTX = BATCH * SEQ
T_LAT = DEC_BATCH * DEC_SEQ
T = T_CTX + T_LAT
N_ASSIGN = T * TOP_K

C_HY = 0
C_DQ = 3 * HY_CH
C_DK = C_DQ + 512
C_DV = C_DK + 512
C_GQ = C_DV + 512
C_GK = C_GQ + 1024
C_GV = C_GK + 256
C_GATE = C_GV + 256
D_IN = C_GATE + 3 * D

LANE = 128
VMEM_LIMIT = 56 * 1024 * 1024

TM_TOK = 512
TM_PROJ = 2048
TN_PROJ = 512
QC = 256
TM_MOE = 512
TN_GU = 512
TM_DN = 256
SLAB = 16
TM_ROUTE = 256
NT_MOE = N_ASSIGN // TM_MOE + N_EXPERTS
NPAD = NT_MOE * TM_MOE


def _params(*sem):
    return pltpu.CompilerParams(dimension_semantics=sem, vmem_limit_bytes=VMEM_LIMIT)


def _mod_row(tile_idx, tm):
    tok0 = tile_idx * tm
    return jnp.where(tok0 < T_CTX, 0, 1 + (tok0 - T_CTX) // DEC_SEQ)


def _adaln_kernel(c_ref, w_ref, b_ref, o_ref):
    c = c_ref[...]
    a = (c * jax.nn.sigmoid(c)).astype(BF16)
    o_ref[...] = jnp.dot(a, w_ref[...].astype(BF16), preferred_element_type=F32) + b_ref[...]


def adaln_all(cond8, w_mod, b_mod):
    tn = 1024
    return pl.pallas_call(
        _adaln_kernel,
        grid=(DEPTH, N_MOD * D // tn),
        in_specs=[pl.BlockSpec((8, D), lambda l, j: (0, 0)),
                  pl.BlockSpec((None, D, tn), lambda l, j: (l, 0, j)),
                  pl.BlockSpec((None, 1, tn), lambda l, j: (l, 0, j))],
        out_specs=pl.BlockSpec((None, 8, tn), lambda l, j: (l, 0, j)),
        out_shape=jax.ShapeDtypeStruct((DEPTH, 8, N_MOD * D), F32),
        compiler_params=_params("arbitrary", "arbitrary"),
        name="adaln",
    )(cond8, w_mod, b_mod.reshape(DEPTH, 1, N_MOD * D))


def _modulated_norm(x, g, sh, sc):
    ms = jnp.mean(x * x, axis=-1, keepdims=True)
    return (x * lax.rsqrt(ms + EPS) * g) * (1.0 + sc) + sh


def _modulate_kernel(x_ref, g_ref, mod_ref, o_ref, *, k_shift):
    r = _mod_row(pl.program_id(0), TM_TOK)
    sh = mod_ref[pl.ds(r, 1), k_shift * D:(k_shift + 1) * D]
    sc = mod_ref[pl.ds(r, 1), (k_shift + 1) * D:(k_shift + 2) * D]
    o_ref[...] = _modulated_norm(x_ref[...], g_ref[...], sh, sc).astype(BF16)


def modulate(x, g, mod, k_shift):
    return pl.pallas_call(
        functools.partial(_modulate_kernel, k_shift=k_shift),
        grid=(T // TM_TOK,),
        in_specs=[pl.BlockSpec((TM_TOK, D), lambda i: (i, 0)),
                  pl.BlockSpec((1, D), lambda i: (0, 0)),
                  pl.BlockSpec((8, N_MOD * D), lambda i: (0, 0))],
        out_specs=pl.BlockSpec((TM_TOK, D), lambda i: (i, 0)),
        out_shape=jax.ShapeDtypeStruct((T, D), BF16),
        compiler_params=_params("arbitrary"),
        name="modulate",
    )(x, g.reshape(1, D), mod)


def _proj_kernel(x_ref, w_ref, o_ref, wbf_ref, *, act):
    @pl.when(pl.program_id(1) == 0)
    def _():
        wbf_ref[...] = w_ref[...].astype(BF16)

    y = jnp.dot(x_ref[...], wbf_ref[...], preferred_element_type=F32)
    if act == "sigmoid":
        y = jax.nn.sigmoid(y)
    o_ref[...] = y.astype(o_ref.dtype)


def proj(h, w, layer, col0, ncols, out_dtype, act, name):
    blk0 = col0 // TN_PROJ
    return pl.pallas_call(
        functools.partial(_proj_kernel, act=act),
        grid=(ncols // TN_PROJ, T // TM_PROJ),
        in_specs=[pl.BlockSpec((TM_PROJ, D), lambda j, i: (i, 0)),
                  pl.BlockSpec((None, D, TN_PROJ), lambda j, i: (layer, 0, blk0 + j))],
        out_specs=pl.BlockSpec((TM_PROJ, TN_PROJ), lambda j, i: (i, j)),
        out_shape=jax.ShapeDtypeStruct((T, ncols), out_dtype),
        scratch_shapes=[pltpu.VMEM((D, TN_PROJ), BF16)],
        compiler_params=_params("arbitrary", "arbitrary"),
        name=name,
    )(h, w)


def _norm_halves(x, g):
    lo = lax.broadcasted_iota(jnp.int32, x.shape, 1) < DF_QK
    x2 = x * x
    s_lo = jnp.sum(jnp.where(lo, x2, 0.0), axis=-1, keepdims=True)
    s_hi = jnp.sum(jnp.where(lo, 0.0, x2), axis=-1, keepdims=True)
    inv = jnp.where(lo, lax.rsqrt(s_lo * (1.0 / DF_QK) + EPS), lax.rsqrt(s_hi * (1.0 / DF_QK) + EPS))
    return x * inv * g


def _norm_full(x, g):
    ms = jnp.mean(x * x, axis=-1, keepdims=True)
    return x * lax.rsqrt(ms + EPS) * g


def _rope(x, cos, sin_a, sin_b, shift):
    return x * cos + pltpu.roll(x, LANE - shift, 1) * sin_a + pltpu.roll(x, shift, 1) * sin_b


def _softmax(s):
    m = jnp.max(s, axis=-1, keepdims=True)
    e = jnp.exp(s - m)
    return e / jnp.sum(e, axis=-1, keepdims=True)


def _dot_nt(a, b):
    return lax.dot_general(a, b, (((1,), (1,)), ((), ())), preferred_element_type=F32)


def _attn_kernel(*refs, lq, has_ctx, out_scale):
    if has_ctx:
        (dq_ref, dk_ref, dv_ref, gq_ref, gk_ref, gv_ref, cdk_ref, cdv_ref, cgk_ref, cgv_ref,
         cd_ref, sad_ref, sbd_ref, cg_ref, sag_ref, sbg_ref, cdq_ref, sadq_ref, sbdq_ref,
         cgq_ref, sagq_ref, sbgq_ref,
         qn_ref, kn_ref, gqn_ref, gkn_ref, sub_ref, lam_ref,
         yb_ref, yc_ref, kd_s, vd_s, kg_s, vg_s) = refs
    else:
        (dq_ref, dk_ref, dv_ref, gq_ref, gk_ref, gv_ref,
         qn_ref, kn_ref, gqn_ref, gkn_ref, sub_ref, lam_ref,
         yb_ref, yc_ref, ndk_ref, ndv_ref, ngk_ref, ngv_ref, kd_s, vd_s, kg_s, vg_s) = refs

    @pl.when(pl.program_id(1) == 0)
    def _():
        for h in range(DF_HEADS):
            cs = slice(h * LANE, (h + 1) * LANE)
            kn = _norm_halves(dk_ref[:, cs], kn_ref[...])
            if has_ctx:
                kn = _rope(kn, cd_ref[...], sad_ref[...], sbd_ref[...], DF_QK // 4)
            else:
                ndk_ref[:, cs] = kn
            kd_s[0:lq, cs] = kn.astype(BF16)
        vd_s[0:lq, :] = dv_ref[...].astype(BF16)
        for g in range(GQ_KV):
            cs = slice(g * LANE, (g + 1) * LANE)
            kn = _norm_full(gk_ref[:, cs], gkn_ref[...])
            if has_ctx:
                kn = _rope(kn, cg_ref[...], sag_ref[...], sbg_ref[...], GQ_DIM // 4)
            else:
                ngk_ref[:, cs] = kn
            kg_s[0:lq, cs] = kn.astype(BF16)
        vg_s[0:lq, :] = gv_ref[...].astype(BF16)
        if has_ctx:
            kd_s[lq:lq + PAST, :] = cdk_ref[...].astype(BF16)
            vd_s[lq:lq + PAST, :] = cdv_ref[...].astype(BF16)
            kg_s[lq:lq + PAST, :] = cgk_ref[...].astype(BF16)
            vg_s[lq:lq + PAST, :] = cgv_ref[...].astype(BF16)
        else:
            ndv_ref[...] = dv_ref[...]
            ngv_ref[...] = gv_ref[...]

    lam = lam_ref[:, 0:1]
    lo = lax.broadcasted_iota(jnp.int32, (QC, LANE), 1) < DF_QK

    for h in range(DF_HEADS):
        cs = slice(h * LANE, (h + 1) * LANE)
        qn = _norm_halves(dq_ref[:, cs], qn_ref[...])
        if has_ctx:
            qn = _rope(qn, cdq_ref[...], sadq_ref[...], sbdq_ref[...], DF_QK // 4)
        qn = qn * (DF_QK ** -0.5)
        kh = kd_s[:, cs]
        p0 = _softmax(_dot_nt(jnp.where(lo, qn, 0.0).astype(BF16), kh))
        p1 = _softmax(_dot_nt(jnp.where(lo, 0.0, qn).astype(BF16), kh))
        w = (p0 - lam * p1).astype(BF16)
        o = jnp.dot(w, vd_s[:, cs], preferred_element_type=F32)
        yb_ref[:, cs] = (_norm_full(o, sub_ref[...]) * out_scale).astype(yb_ref.dtype)

    for hq in range(GQ_HEADS):
        g = hq // (GQ_HEADS // GQ_KV)
        cs = slice(hq * LANE, (hq + 1) * LANE)
        ks = slice(g * LANE, (g + 1) * LANE)
        qn = _norm_full(gq_ref[:, cs], gqn_ref[...])
        if has_ctx:
            qn = _rope(qn, cgq_ref[...], sagq_ref[...], sbgq_ref[...], GQ_DIM // 4)
        s = _dot_nt(qn.astype(BF16), kg_s[:, ks]) * (GQ_DIM ** -0.5)
        p = _softmax(s).astype(BF16)
        yc_ref[:, cs] = jnp.dot(p, vg_s[:, ks], preferred_element_type=F32).astype(yc_ref.dtype)


def _rope_tables(d_head):
    t = np.arange(DEC_SEQ)
    row, col = (t // GRID_W).astype(np.float64), (t % GRID_W).astype(np.float64)
    q = d_head // 4
    inv = ROPE_THETA ** (-np.arange(q, dtype=np.float64) / q)
    ar, ac = row[:, None] * inv, col[:, None] * inv
    z = np.zeros_like(ar)
    cos = np.concatenate([np.cos(ar), np.cos(ar), np.cos(ac), np.cos(ac)], -1)
    sin_a = np.concatenate([-np.sin(ar), z, -np.sin(ac), z], -1)
    sin_b = np.concatenate([z, np.sin(ar), z, np.sin(ac)], -1)
    rep = LANE // d_head
    return tuple(jnp.asarray(np.tile(a, (1, rep)), F32) for a in (cos, sin_a, sin_b))


def attention(p1, gains, lam, layer_idx, caches=None):
    has_ctx = caches is not None
    lq = DEC_SEQ if has_ctx else SEQ
    n_seq = DEC_BATCH if has_ctx else BATCH
    row0 = (T_CTX // lq) if has_ctx else 0
    lk = lq + (PAST if has_ctx else 0)
    nq = lq // QC
    lam_init = 0.8 - 0.6 * math.exp(-0.3 * layer_idx)

    def qspec(width, col0):
        return pl.BlockSpec((QC, width), lambda b, q: ((row0 + b) * nq + q, col0 // width))

    def kspec(width, col0):
        return pl.BlockSpec((lq, width), lambda b, q: (row0 + b, col0 // width))

    const = lambda shape: pl.BlockSpec(shape, lambda b, q: (0,) * len(shape))
    in_specs = [qspec(512, C_DQ), kspec(512, C_DK), kspec(512, C_DV),
                qspec(1024, C_GQ), kspec(256, C_GK), kspec(256, C_GV)]
    args = [p1] * 6
    if has_ctx:
        cdk, cdv, cgk, cgv = caches
        in_specs += [pl.BlockSpec((None, PAST, 512), lambda b, q: (b, 0, 0)),
                     pl.BlockSpec((None, PAST, 512), lambda b, q: (b, 0, 0)),
                     pl.BlockSpec((None, PAST, 256), lambda b, q: (b, 0, 0)),
                     pl.BlockSpec((None, PAST, 256), lambda b, q: (b, 0, 0))]
        args += [cdk, cdv, cgk, cgv]
        td, tg = _rope_tables(DF_QK), _rope_tables(GQ_DIM)
        in_specs += [const((lq, LANE))] * 6 + [pl.BlockSpec((QC, LANE), lambda b, q: (q, 0))] * 6
        args += list(td) + list(tg) + list(td) + list(tg)
    in_specs += [const((1, LANE))] * 6
    args += list(gains) + [jnp.full((1, LANE), lam, F32)]

    out_specs = [pl.BlockSpec((QC, 512), lambda b, q: ((row0 + b) * nq + q - row0 * nq, 0)),
                 pl.BlockSpec((QC, 1024), lambda b, q: ((row0 + b) * nq + q - row0 * nq, 0))]
    n_rows = n_seq * lq
    out_shape = [jax.ShapeDtypeStruct((n_rows, 512), BF16), jax.ShapeDtypeStruct((n_rows, 1024), BF16)]
    if not has_ctx:
        out_specs += [pl.BlockSpec((lq, 512), lambda b, q: (b, 0)), pl.BlockSpec((lq, 512), lambda b, q: (b, 0)),
                      pl.BlockSpec((lq, 256), lambda b, q: (b, 0)), pl.BlockSpec((lq, 256), lambda b, q: (b, 0))]
        out_shape += [jax.ShapeDtypeStruct((n_rows, 512), F32), jax.ShapeDtypeStruct((n_rows, 512), F32),
                      jax.ShapeDtypeStruct((n_rows, 256), F32), jax.ShapeDtypeStruct((n_rows, 256), F32)]
    return pl.pallas_call(
        functools.partial(_attn_kernel, lq=lq, has_ctx=has_ctx, out_scale=1.0 - lam_init),
        grid=(n_seq, nq),
        in_specs=in_specs,
        out_specs=out_specs,
        out_shape=out_shape,
        scratch_shapes=[pltpu.VMEM((lk, 512), BF16), pltpu.VMEM((lk, 512), BF16),
                        pltpu.VMEM((lk, 256), BF16), pltpu.VMEM((lk, 256), BF16)],
        compiler_params=_params("arbitrary", "arbitrary"),
        name="attn_latent" if has_ctx else "attn_context",
    )(*args)


def _short_conv(u, w_ref, b_ref, length):
    row = lax.broadcasted_iota(jnp.int32, u.shape, 0)
    prev = jnp.where(row == 0, 0.0, pltpu.roll(u, 1, 0))
    nxt = jnp.where(row == length - 1, 0.0, pltpu.roll(u, length - 1, 0))
    return w_ref[0:1, :] * prev + w_ref[1:2, :] * u + w_ref[2:3, :] * nxt + b_ref[...]


def _hyena_kernel(v_ref, x_ref, cwv_ref, cbv_ref, cwx_ref, cbx_ref, skip_ref, fm_ref, gm_ref,
                  a_ref, b_ref, a2_ref, o_ref, z_s, *, length):
    n = pl.program_id(2)

    @pl.when(n == 0)
    def _():
        z_s[...] = _short_conv(v_ref[...], cwv_ref, cbv_ref, length)

    z = z_s[...]
    xn = _short_conv(x_ref[...], cwx_ref, cbx_ref, length)
    u = jnp.dot(fm_ref[...], z.astype(BF16), preferred_element_type=F32)
    ur, ui = u[:length], u[length:]
    a, b, a2 = a_ref[...], b_ref[...], a2_ref[...]
    yr = (ur * a - ui * b).astype(BF16)
    yi = (ur * b + ui * a2).astype(BF16)
    conv = (jnp.dot(gm_ref[:, :length], yr, preferred_element_type=F32)
            + jnp.dot(gm_ref[:, length:], yi, preferred_element_type=F32))
    z = xn * (conv + skip_ref[...] * z)
    z_s[...] = z
    o_ref[...] = z.astype(o_ref.dtype)


def _dft_matrices(length):
    n = 2 * length
    f = np.arange(length)[:, None]
    t = np.arange(length)[None, :]
    ang = ((f * t) % n).astype(np.float64) * (2.0 * math.pi / n)
    cos, sin = np.cos(ang), np.sin(ang)
    alt = np.broadcast_to(np.where(t % 2 == 0, 1.0, -1.0), sin.shape)
    fm = np.concatenate([cos, np.where(f == 0, alt, -sin)], axis=0)
    wre = np.where(f == 0, 1.0 / n, 2.0 / n) * cos
    wim = np.where(f == 0, alt / n, -(2.0 / n) * sin)
    gm = np.concatenate([wre, wim], axis=0).T
    fm32, gm32 = jnp.asarray(fm, F32), jnp.asarray(gm, F32)
    fm_hi, fm_lo = _split3(fm32)
    return fm_hi, fm_lo, gm32.astype(BF16)


def _dot3(a_hi, a_lo, b):
    b_hi, b_lo = _split3(b)
    return (jnp.dot(a_hi, b_hi, preferred_element_type=F32) + jnp.dot(a_hi, b_lo, preferred_element_type=F32)
            + jnp.dot(a_lo, b_hi, preferred_element_type=F32))


def _spectrum_kernel(kp_ref, km_ref, fhi_ref, flo_ref, a_ref, b_ref, a2_ref, *, length):
    p = _dot3(fhi_ref[...], flo_ref[...], kp_ref[...])
    q = _dot3(fhi_ref[...], flo_ref[...], km_ref[...])
    first = lax.broadcasted_iota(jnp.int32, (length, p.shape[1]), 0) == 0
    a = p[:length]
    a_ref[...] = a
    b_ref[...] = jnp.where(first, 0.0, q[length:])
    a2_ref[...] = jnp.where(first, p[length:length + 1], a)


def _hyena_filter_spectra(length, dft, w1, b1, w2, b2, w3, freq):
    hp = lax.Precision.HIGHEST
    t = jnp.linspace(0.0, 1.0, length, dtype=F32)[:, None]
    bands = jnp.linspace(1e-4, HY_BANDS - 1, HY_BANDS, dtype=F32)
    wpos = (2 * math.pi / length) * jnp.arange(length, dtype=F32)[:, None] * bands
    z = jnp.concatenate([t, jnp.cos(wpos), -jnp.sin(wpos)], -1)
    h = jnp.sin(freq * (jnp.dot(z, w1, precision=hp) + b1))
    h = jnp.sin(freq * (jnp.dot(h, w2, precision=hp) + b2))
    h = jnp.dot(h, w3, precision=hp).reshape(length, 2, 2, HY_CH)
    max_decay = math.log(1e-2) / 0.3
    min_decay = math.log(1e-2) / 1.5
    deltas = jnp.linspace(min_decay, max_decay, HY_CH, dtype=F32)
    h = h * jnp.exp(-t[:, :, None, None] * jnp.abs(deltas))
    fwd = h[:, :, 0]
    bwd = jnp.where(jnp.arange(length)[:, None, None] == 0, 0.0, h[:, :, 1])
    kp = (fwd + bwd).reshape(length, 2 * HY_CH)
    km = (fwd - bwd).reshape(length, 2 * HY_CH)
    fm_hi, fm_lo, _ = dft
    cw = 256
    ncb = HY_CH // cw
    in_blk = pl.BlockSpec((length, cw), lambda n, c: (0, n * ncb + c))
    const = pl.BlockSpec((2 * length, length), lambda n, c: (0, 0))
    out_blk = pl.BlockSpec((None, length, cw), lambda n, c: (n, 0, c))
    shape = jax.ShapeDtypeStruct((2, length, HY_CH), F32)
    return pl.pallas_call(
        functools.partial(_spectrum_kernel, length=length),
        grid=(2, ncb),
        in_specs=[in_blk, in_blk, const, const],
        out_specs=[out_blk, out_blk, out_blk],
        out_shape=[shape, shape, shape],
        compiler_params=_params("arbitrary", "arbitrary"),
        name=f"hyena_spectrum_{length}",
    )(kp, km, fm_hi, fm_lo)


def hyena(p1, conv_w, conv_b, skip, spectra, dft, length, n_seq, row0, cw):
    fm, _, gm = dft
    a, b, a2 = spectra
    ncb = HY_CH // cw
    vblk = 2 * ncb
    const = lambda shape: pl.BlockSpec(shape, lambda s, c, n: (0,) * len(shape))
    spec_spec = pl.BlockSpec((None, length, cw), lambda s, c, n: (n, 0, c))
    return pl.pallas_call(
        functools.partial(_hyena_kernel, length=length),
        grid=(n_seq, ncb, 2),
        in_specs=[pl.BlockSpec((length, cw), lambda s, c, n: (row0 + s, vblk + c)),
                  pl.BlockSpec((length, cw), lambda s, c, n: (row0 + s, n * ncb + c)),
                  pl.BlockSpec((3, cw), lambda s, c, n: (0, vblk + c)),
                  pl.BlockSpec((1, cw), lambda s, c, n: (0, vblk + c)),
                  pl.BlockSpec((3, cw), lambda s, c, n: (0, n * ncb + c)),
                  pl.BlockSpec((1, cw), lambda s, c, n: (0, n * ncb + c)),
                  pl.BlockSpec((None, 1, cw), lambda s, c, n: (n, 0, c)),
                  const((2 * length, length)), const((length, 2 * length)),
                  spec_spec, spec_spec, spec_spec],
        out_specs=pl.BlockSpec((length, cw), lambda s, c, n: (s, c)),
        out_shape=jax.ShapeDtypeStruct((n_seq * length, HY_CH), BF16),
        scratch_shapes=[pltpu.VMEM((length, cw), F32)],
        compiler_params=_params("arbitrary", "arbitrary", "arbitrary"),
        name=f"hyena_{length}",
    )(p1, p1, conv_w, conv_b.reshape(1, -1), conv_w, conv_b.reshape(1, -1),
      skip.reshape(2, 1, HY_CH), fm, gm, a, b, a2)


def _merge_kernel(ya_ref, yb_ref, yc_ref, g_ref, wa_ref, wb_ref, wc_ref, o_ref):
    a = jnp.dot(ya_ref[...], wa_ref[...], preferred_element_type=F32)
    b = jnp.dot(yb_ref[...], wb_ref[...], preferred_element_type=F32)
    c = jnp.dot(yc_ref[...], wc_ref[...], preferred_element_type=F32)
    m = g_ref[:, 0:D] * a + g_ref[:, D:2 * D] * b + g_ref[:, 2 * D:3 * D] * c
    o_ref[...] = m.astype(o_ref.dtype)


def merge(ya, yb, yc, gates, wa, wb, wc):
    tm = 256
    const = lambda shape: pl.BlockSpec(shape, lambda i: (0, 0))
    return pl.pallas_call(
        _merge_kernel,
        grid=(T // tm,),
        in_specs=[pl.BlockSpec((tm, 512), lambda i: (i, 0)), pl.BlockSpec((tm, 512), lambda i: (i, 0)),
                  pl.BlockSpec((tm, 1024), lambda i: (i, 0)), pl.BlockSpec((tm, 3 * D), lambda i: (i, 0)),
                  const((512, D)), const((512, D)), const((1024, D))],
        out_specs=pl.BlockSpec((tm, D), lambda i: (i, 0)),
        out_shape=jax.ShapeDtypeStruct((T, D), BF16),
        compiler_params=_params("arbitrary"),
        name="merge",
    )(ya, yb, yc, gates, wa, wb, wc)


def _split3(x):
    hi = x.astype(BF16)
    return hi, (x - hi.astype(F32)).astype(BF16)


def _out_router_kernel(m_ref, x_ref, wout_ref, mod_ref, g_ref, rw_ref, rb_ref,
                       x1_ref, h2_ref, route_ref, wt_ref, cnt_ref, cnt_s, *, tm):
    r = _mod_row(pl.program_id(0), tm)
    o = jnp.dot(m_ref[...], wout_ref[...], preferred_element_type=F32)
    x1 = x_ref[...] + mod_ref[pl.ds(r, 1), 2 * D:3 * D] * o
    x1_ref[...] = x1
    h2 = _modulated_norm(x1, g_ref[...], mod_ref[pl.ds(r, 1), 3 * D:4 * D], mod_ref[pl.ds(r, 1), 4 * D:5 * D])
    for s in range(SLAB):
        h2_ref[pl.ds(s, tm, stride=SLAB), :] = h2[:, s * LANE:(s + 1) * LANE]

    h_hi, h_lo = _split3(h2)
    w_hi, w_lo = _split3(rw_ref[...])
    logits = (jnp.dot(h_hi, w_hi, preferred_element_type=F32) + jnp.dot(h_hi, w_lo, preferred_element_type=F32)
              + jnp.dot(h_lo, w_hi, preferred_element_type=F32)) + rb_ref[...]

    lane = lax.broadcasted_iota(jnp.int32, logits.shape, 1)
    vals, idxs = [], []
    for _ in range(TOP_K):
        mx = jnp.max(logits, axis=-1, keepdims=True)
        ix = jnp.min(jnp.where(logits == mx, lane, LANE), axis=-1, keepdims=True)
        vals.append(mx)
        idxs.append(ix)
        logits = jnp.where(lane == ix, -jnp.inf, logits)
    es = [jnp.exp(v - vals[0]) for v in vals]
    inv = 1.0 / (es[0] + es[1] + es[2] + es[3])

    @pl.when(pl.program_id(0) == 0)
    def _():
        cnt_s[...] = jnp.zeros_like(cnt_s)

    chosen = jnp.zeros(lane.shape, F32)
    for k in range(TOP_K):
        chosen = jnp.where(lane == idxs[k], 1.0, chosen)
    earlier = (lax.broadcasted_iota(jnp.int32, (tm, tm), 0) > lax.broadcasted_iota(jnp.int32, (tm, tm), 1))
    before = jnp.dot(earlier.astype(BF16), chosen.astype(BF16), preferred_element_type=F32) + cnt_s[...]
    route_out = jnp.zeros(lane.shape, jnp.int32)
    wt_out = jnp.zeros(lane.shape, F32)
    for k in range(TOP_K):
        rank = jnp.sum(jnp.where(lane == idxs[k], before, 0.0), axis=-1, keepdims=True).astype(jnp.int32)
        route_out = jnp.where(lane == k, idxs[k], route_out)
        route_out = jnp.where(lane == TOP_K + k, rank, route_out)
        wt_out = jnp.where(lane == k, es[k] * inv, wt_out)
    route_ref[...] = route_out
    wt_ref[...] = wt_out
    total = cnt_s[...] + jnp.sum(chosen, axis=0, keepdims=True)
    cnt_s[...] = total
    cnt_ref[...] = total.astype(jnp.int32)


def out_router(merged, x, w_out, mod, norm2_g, router_w, router_b):
    tm = TM_ROUTE
    const = lambda shape: pl.BlockSpec(shape, lambda i: (0, 0))
    rw = jnp.zeros((D, LANE), F32).at[:, :N_EXPERTS].set(router_w)
    rb = jnp.full((1, LANE), -1e30, F32).at[0, :N_EXPERTS].set(router_b)
    return pl.pallas_call(
        functools.partial(_out_router_kernel, tm=tm),
        grid=(T // tm,),
        in_specs=[pl.BlockSpec((tm, D), lambda i: (i, 0)), pl.BlockSpec((tm, D), lambda i: (i, 0)),
                  const((D, D)), const((8, N_MOD * D)), const((1, D)), const((D, LANE)), const((1, LANE))],
        out_specs=[pl.BlockSpec((tm, D), lambda i: (i, 0)), pl.BlockSpec((tm * SLAB, LANE), lambda i: (i, 0)),
                   pl.BlockSpec((tm, LANE), lambda i: (i, 0)), pl.BlockSpec((tm, LANE), lambda i: (i, 0)),
                   const((1, LANE))],
        out_shape=[jax.ShapeDtypeStruct((T, D), F32), jax.ShapeDtypeStruct((T * SLAB, LANE), F32),
                   jax.ShapeDtypeStruct((T, LANE), jnp.int32), jax.ShapeDtypeStruct((T, LANE), F32),
                   jax.ShapeDtypeStruct((1, LANE), jnp.int32)],
        scratch_shapes=[pltpu.VMEM((1, LANE), F32)],
        compiler_params=_params("arbitrary"),
        name="out_router",
    )(merged, x, w_out, mod, norm2_g.reshape(1, D), rw, rb)


def _segments(counts):
    padded = ((counts + TM_MOE - 1) // TM_MOE) * TM_MOE
    seg_end = jnp.cumsum(padded)
    return padded, seg_end - padded, seg_end


def _step_plan(counts, n_col_blocks, tm):
    padded, seg_start, seg_end = _segments(counts)
    n_tiles = seg_end[-1] // tm
    step = jnp.arange((NPAD // tm) * n_col_blocks, dtype=jnp.int32)
    tile_probe = jnp.minimum(step // n_col_blocks, n_tiles - 1)
    ends_before = (seg_end[None, :] <= (tile_probe * tm)[:, None]).astype(jnp.int32)
    ex = jnp.minimum(jnp.sum(ends_before, axis=1), N_EXPERTS - 1)
    onehot = (ex[:, None] == jnp.arange(N_EXPERTS, dtype=jnp.int32)[None, :]).astype(jnp.int32)
    t0 = jnp.sum(onehot * seg_start[None, :], axis=1) // tm
    ne = jnp.maximum(jnp.sum(onehot * padded[None, :], axis=1) // tm, 1)
    valid = step < n_tiles * n_col_blocks
    local = jnp.where(valid, step - n_col_blocks * t0, n_col_blocks * ne - 1)
    col = local // ne
    tile = t0 + local % ne
    first = jnp.logical_and(valid, local % ne == 0)
    rows = jnp.clip(jnp.sum(onehot * counts[None, :], axis=1) - (tile - t0) * tm, 0, tm)
    last_col = col == n_col_blocks - 1
    ends_before_next = (seg_end[None, :] <= ((t0 + ne) * tm)[:, None]).astype(jnp.int32)
    next_ex = jnp.where(last_col, jnp.minimum(jnp.sum(ends_before_next, axis=1), N_EXPERTS - 1), ex)
    next_col = jnp.where(last_col, 0, col + 1)
    next_ok = jnp.logical_and(first, jnp.logical_or(~last_col, t0 + ne < n_tiles))
    lower = (jnp.arange(N_EXPERTS, dtype=jnp.int32)[None, :] < ex[:, None]).astype(jnp.int32)
    blocks_before = n_col_blocks * jnp.sum(lower * (padded > 0).astype(jnp.int32)[None, :], axis=1) + col
    i32 = lambda a: a.astype(jnp.int32)
    return dict(ex=i32(ex), col=i32(col), tile=i32(tile), first=i32(first), valid=i32(valid), rows=i32(rows),
                next_ex=i32(next_ex), next_col=i32(next_col), next_ok=i32(next_ok), slot=i32(blocks_before % 2))


def _dispatch_kernel(pos_ref, h_ref, xs_hbm, sem):
    def issue(t, carry):
        src = pl.multiple_of(t * SLAB, SLAB)
        for k in range(TOP_K):
            dst = pl.multiple_of(pos_ref[0, t * TOP_K + k] * SLAB, SLAB)
            pltpu.make_async_copy(h_ref.at[pl.ds(src, SLAB), :], xs_hbm.at[pl.ds(dst, SLAB), :],
                                  sem).start(priority=k % 2)
        return carry

    lax.fori_loop(0, TM_ROUTE, issue, 0)
    for _ in range(TOP_K):
        pltpu.make_async_copy(h_ref, xs_hbm.at[pl.ds(0, TM_ROUTE * SLAB), :], sem).wait()


def dispatch(h2_slab, pos):
    n_steps = T // TM_ROUTE
    return pl.pallas_call(
        _dispatch_kernel,
        grid=(n_steps,),
        in_specs=[pl.BlockSpec((None, 1, TM_ROUTE * TOP_K), lambda i: (i, 0, 0), memory_space=pltpu.SMEM),
                  pl.BlockSpec((TM_ROUTE * SLAB, LANE), lambda i: (i, 0))],
        out_specs=pl.BlockSpec(memory_space=pl.ANY),
        out_shape=jax.ShapeDtypeStruct((NPAD * SLAB, LANE), F32),
        scratch_shapes=[pltpu.SemaphoreType.DMA(())],
        compiler_params=_params("arbitrary"),
        name="moe_dispatch",
    )(pos.reshape(n_steps, 1, TM_ROUTE * TOP_K), h2_slab)


_PLAN_KEYS = ("ex", "col", "tile", "first", "valid", "rows", "next_ex", "next_col", "next_ok", "slot")


def _stream_weights(s, plan, fetch, on_arrival):
    @pl.when(s == 0)
    def _():
        for cp in fetch(plan["ex"][0], plan["col"][0], 0):
            cp.start()

    @pl.when(plan["first"][s] == 1)
    def _():
        slot = plan["slot"][s]
        for cp in fetch(plan["ex"][s], plan["col"][s], slot):
            cp.wait()

        @pl.when(plan["next_ok"][s] == 1)
        def _():
            for cp in fetch(plan["next_ex"][s], plan["next_col"][s], 1 - slot):
                cp.start()

        on_arrival(slot)


def _gate_up_kernel(*refs, layer):
    plan = dict(zip(_PLAN_KEYS, refs[:len(_PLAN_KEYS)]))
    x_ref, w_hbm, bg_ref, bu_ref, o_ref, stage, wg_s, wu_s, sem = refs[len(_PLAN_KEYS):]
    s = pl.program_id(0)

    def fetch(e, c, slot):
        c0 = pl.multiple_of(c * TN_GU, TN_GU)
        return [pltpu.make_async_copy(w_hbm.at[layer, e, :, pl.ds(c0, TN_GU)], stage.at[slot, 0], sem.at[slot]),
                pltpu.make_async_copy(w_hbm.at[layer, e, :, pl.ds(D_FF + c0, TN_GU)], stage.at[slot, 1], sem.at[slot])]

    def on_arrival(slot):
        wg_s[...] = stage[slot, 0].astype(BF16)
        wu_s[...] = stage[slot, 1].astype(BF16)

    _stream_weights(s, plan, fetch, on_arrival)

    def compute(rows):
        g = jnp.broadcast_to(bg_ref[...], (rows, TN_GU))
        u = jnp.broadcast_to(bu_ref[...], (rows, TN_GU))
        for c in range(SLAB // 2):
            xc = jnp.concatenate([x_ref[pl.ds(2 * c, rows, stride=SLAB), :],
                                  x_ref[pl.ds(2 * c + 1, rows, stride=SLAB), :]], axis=1).astype(BF16)
            ks = slice(c * 2 * LANE, (c + 1) * 2 * LANE)
            g = g + jnp.dot(xc, wg_s[ks, :], preferred_element_type=F32)
            u = u + jnp.dot(xc, wu_s[ks, :], preferred_element_type=F32)
        gate = jnp.minimum(g, SWIGLU_LIMIT)
        up = jnp.clip(u, -SWIGLU_LIMIT, SWIGLU_LIMIT)
        o_ref[0:rows, :] = (gate * jax.nn.sigmoid(SWIGLU_ALPHA * gate) * (up + 1.0)).astype(o_ref.dtype)

    live = plan["valid"][s] == 1
    half = plan["rows"][s] <= TM_MOE // 2

    @pl.when(jnp.logical_and(live, jnp.logical_not(half)))
    def _():
        compute(TM_MOE)

    @pl.when(jnp.logical_and(live, half))
    def _():
        compute(TM_MOE // 2)


def gate_up(x_sorted, w_gu, b_gu, layer, plan):
    ncb = D_FF // TN_GU
    n = len(_PLAN_KEYS)
    imap = lambda f: (lambda s, *p: f(s, dict(zip(_PLAN_KEYS, p))))
    grid_spec = pltpu.PrefetchScalarGridSpec(
        num_scalar_prefetch=n,
        grid=(NT_MOE * ncb,),
        in_specs=[pl.BlockSpec((TM_MOE * SLAB, LANE), imap(lambda s, p: (p["tile"][s], 0))),
                  pl.BlockSpec(memory_space=pl.ANY),
                  pl.BlockSpec((None, None, 1, TN_GU), imap(lambda s, p: (layer, p["ex"][s], 0, p["col"][s]))),
                  pl.BlockSpec((None, None, 1, TN_GU), imap(lambda s, p: (layer, p["ex"][s], 0, ncb + p["col"][s])))],
        out_specs=pl.BlockSpec((TM_MOE, TN_GU), imap(lambda s, p: (p["tile"][s], p["col"][s]))),
        scratch_shapes=[pltpu.VMEM((2, 2, D, TN_GU), F32), pltpu.VMEM((D, TN_GU), BF16), pltpu.VMEM((D, TN_GU), BF16),
                        pltpu.SemaphoreType.DMA((2,))],
    )
    b4 = b_gu.reshape(DEPTH, N_EXPERTS, 1, 2 * D_FF)
    return pl.pallas_call(
        functools.partial(_gate_up_kernel, layer=layer),
        grid_spec=grid_spec,
        out_shape=jax.ShapeDtypeStruct((NPAD, D_FF), BF16),
        compiler_params=_params("arbitrary"),
        name="moe_gate_up",
    )(*[plan[k] for k in _PLAN_KEYS], x_sorted, w_gu, b4, b4)


def _down_kernel(*refs, layer):
    plan = dict(zip(_PLAN_KEYS, refs[:len(_PLAN_KEYS)]))
    a_ref, w_hbm, b_ref, o_ref, stage, w_s, sem = refs[len(_PLAN_KEYS):]
    s = pl.program_id(0)

    def fetch(e, c, slot):
        return [pltpu.make_async_copy(w_hbm.at[layer, e], stage.at[slot], sem.at[slot])]

    def on_arrival(slot):
        w_s[...] = stage[slot].astype(BF16)

    _stream_weights(s, plan, fetch, on_arrival)

    @pl.when(jnp.logical_and(plan["valid"][s] == 1, plan["rows"][s] > 0))
    def _():
        y = jnp.dot(a_ref[...], w_s[...], preferred_element_type=F32) + b_ref[...]
        for r in range(SLAB):
            o_ref[pl.ds(r, TM_DN, stride=SLAB), :] = y[:, r * LANE:(r + 1) * LANE]


def down(act, w_dn, b_dn, layer, plan):
    n = len(_PLAN_KEYS)
    imap = lambda f: (lambda s, *p: f(s, dict(zip(_PLAN_KEYS, p))))
    grid_spec = pltpu.PrefetchScalarGridSpec(
        num_scalar_prefetch=n,
        grid=(NPAD // TM_DN,),
        in_specs=[pl.BlockSpec((TM_DN, D_FF), imap(lambda s, p: (p["tile"][s], 0))),
                  pl.BlockSpec(memory_space=pl.ANY),
                  pl.BlockSpec((None, None, 1, D), imap(lambda s, p: (layer, p["ex"][s], 0, 0)))],
        out_specs=pl.BlockSpec((TM_DN * SLAB, LANE), imap(lambda s, p: (p["tile"][s], 0))),
        scratch_shapes=[pltpu.VMEM((2, D_FF, D), F32), pltpu.VMEM((D_FF, D), BF16), pltpu.SemaphoreType.DMA((2,))],
    )
    return pl.pallas_call(
        functools.partial(_down_kernel, layer=layer),
        grid_spec=grid_spec,
        out_shape=jax.ShapeDtypeStruct((NPAD * SLAB, LANE), F32),
        compiler_params=_params("arbitrary"),
        name="moe_down",
    )(*[plan[k] for k in _PLAN_KEYS], act, w_dn, b_dn.reshape(DEPTH, N_EXPERTS, 1, D))


def _combine_kernel(pos_ref, ys_hbm, x_ref, wt_ref, mod_ref, o_ref, buf, sem):
    i = pl.program_id(0)
    tm = TM_ROUTE

    def issue(t, carry):
        for k in range(TOP_K):
            src = pl.multiple_of(pos_ref[0, t * TOP_K + k] * SLAB, SLAB)
            dst = pl.multiple_of((k * tm + t) * SLAB, SLAB)
            pltpu.make_async_copy(ys_hbm.at[pl.ds(src, SLAB), :], buf.at[pl.ds(dst, SLAB), :],
                                  sem).start(priority=k % 2)
        return carry

    lax.fori_loop(0, tm, issue, 0)
    pltpu.make_async_copy(ys_hbm.at[pl.ds(0, TOP_K * tm * SLAB), :], buf, sem).wait()

    gate_row = mod_ref[pl.ds(_mod_row(i, tm), 1), 5 * D:6 * D]
    wk = [wt_ref[:, k:k + 1] for k in range(TOP_K)]
    for s in range(SLAB):
        acc = jnp.zeros((tm, LANE), F32)
        for k in range(TOP_K):
            acc = acc + wk[k] * buf[pl.ds(k * tm * SLAB + s, tm, stride=SLAB), :]
        cs = slice(s * LANE, (s + 1) * LANE)
        o_ref[:, cs] = x_ref[:, cs] + gate_row[:, cs] * acc


def combine(ys, pos, wts, x1, mod):
    n_steps = T // TM_ROUTE
    return pl.pallas_call(
        _combine_kernel,
        grid=(n_steps,),
        in_specs=[pl.BlockSpec((None, 1, TM_ROUTE * TOP_K), lambda i: (i, 0, 0), memory_space=pltpu.SMEM),
                  pl.BlockSpec(memory_space=pl.ANY),
                  pl.BlockSpec((TM_ROUTE, D), lambda i: (i, 0)),
                  pl.BlockSpec((TM_ROUTE, LANE), lambda i: (i, 0)),
                  pl.BlockSpec((8, N_MOD * D), lambda i: (0, 0))],
        out_specs=pl.BlockSpec((TM_ROUTE, D), lambda i: (i, 0)),
        out_shape=jax.ShapeDtypeStruct((T, D), F32),
        scratch_shapes=[pltpu.VMEM((TOP_K * TM_ROUTE * SLAB, LANE), F32), pltpu.SemaphoreType.DMA(())],
        compiler_params=_params("arbitrary"),
        name="moe_combine",
    )(pos.reshape(n_steps, 1, TM_ROUTE * TOP_K), ys, x1, wts, mod)


def kernel(x_prompt, x_sample, cache_diff_k, cache_diff_v, cache_gqa_k, cache_gqa_v, c, c_ctx, w_mod, b_mod, norm1_g, norm2_g, w_in, hy_conv_w, hy_conv_b, hy_f_w1, hy_f_b1, hy_f_w2, hy_f_b2, hy_f_w3, hy_freq, hy_skip, df_qn_g, df_kn_g, df_lq1, df_lk1, df_lq2, df_lk2, df_subln_g, gq_qn_g, gq_kn_g, w_hy_o, w_df_o, w_gq_o, w_out, router_w, router_b, w_gu, b_gu, w_dn, b_dn):
    x = jnp.concatenate([x_prompt.reshape(T_CTX, D), x_sample.reshape(T_LAT, D)], axis=0)
    cond8 = jnp.zeros((8, D), F32).at[0].set(c_ctx).at[1:1 + DEC_BATCH].set(c)
    mods = adaln_all(cond8, w_mod, b_mod)

    cdk = cache_diff_k.reshape(DEC_BATCH, DEPTH, PAST, 512)
    cdv = cache_diff_v.reshape(DEC_BATCH, DEPTH, PAST, 512)
    cgk = cache_gqa_k.reshape(DEC_BATCH, DEPTH, PAST, 256)
    cgv = cache_gqa_v.reshape(DEC_BATCH, DEPTH, PAST, 256)

    dft_c, dft_l = _dft_matrices(SEQ), _dft_matrices(DEC_SEQ)

    new_dk, new_dv, new_gk, new_gv = [], [], [], []
    for l in range(DEPTH):
        mod = mods[l]
        h = modulate(x, norm1_g[l], mod, 0)
        p1 = proj(h, w_in, l, 0, C_GATE, F32, None, "proj_mix")
        gates = proj(h, w_in, l, C_GATE, 3 * D, BF16, "sigmoid", "proj_gates")

        tile128 = lambda g: jnp.tile(g, LANE // g.shape[0]).reshape(1, LANE)
        gains = (tile128(df_qn_g[l]), tile128(df_kn_g[l]), tile128(gq_qn_g[l]), tile128(gq_kn_g[l]),
                 tile128(df_subln_g[l]))
        lam_init = 0.8 - 0.6 * math.exp(-0.3 * l)
        lam = (jnp.exp(jnp.sum(df_lq1[l] * df_lk1[l])) - jnp.exp(jnp.sum(df_lq2[l] * df_lk2[l])) + lam_init)

        yb_c, yc_c, ndk, ndv, ngk, ngv = attention(p1, gains, lam, l)
        yb_l, yc_l = attention(p1, gains, lam, l, caches=(cdk[:, l], cdv[:, l], cgk[:, l], cgv[:, l]))
        new_dk.append(ndk.reshape(BATCH, SEQ, DF_HEADS, 2, DF_QK))
        new_dv.append(ndv.reshape(BATCH, SEQ, DF_HEADS, DF_V))
        new_gk.append(ngk.reshape(BATCH, SEQ, GQ_KV, GQ_DIM))
        new_gv.append(ngv.reshape(BATCH, SEQ, GQ_KV, GQ_DIM))

        filt = (hy_f_w1[l], hy_f_b1[l], hy_f_w2[l], hy_f_b2[l], hy_f_w3[l], hy_freq[l])
        ya_c = hyena(p1, hy_conv_w[l], hy_conv_b[l], hy_skip[l], _hyena_filter_spectra(SEQ, dft_c, *filt),
                     dft_c, SEQ, BATCH, 0, 512)
        ya_l = hyena(p1, hy_conv_w[l], hy_conv_b[l], hy_skip[l], _hyena_filter_spectra(DEC_SEQ, dft_l, *filt),
                     dft_l, DEC_SEQ, DEC_BATCH, T_CTX // DEC_SEQ, 256)

        ya = jnp.concatenate([ya_c, ya_l], axis=0)
        yb = jnp.concatenate([yb_c, yb_l], axis=0)
        yc = jnp.concatenate([yc_c, yc_l], axis=0)
        merged = merge(ya, yb, yc, gates, w_hy_o[l].astype(BF16), w_df_o[l].astype(BF16), w_gq_o[l].astype(BF16))
        x1, h2_slab, route, wts, cnt = out_router(merged, x, w_out[l].astype(BF16), mod, norm2_g[l],
                                                  router_w[l], router_b[l])

        counts = cnt[0, :N_EXPERTS]
        seg_start = _segments(counts)[1]
        pos = seg_start[route[:, :TOP_K]] + route[:, TOP_K:2 * TOP_K]
        x_sorted = dispatch(h2_slab, pos)
        act = gate_up(x_sorted, w_gu, b_gu, l, _step_plan(counts, D_FF // TN_GU, TM_MOE))
        ys = down(act, w_dn, b_dn, l, _step_plan(counts, 1, TM_DN))
        x = combine(ys, pos, wts, x1, mod)

    y_prompt = x[:T_CTX].reshape(BATCH, SEQ, D)
    y_sample = x[T_CTX:].reshape(DEC_BATCH, DEC_SEQ, D)
    return (y_prompt, y_sample, jnp.stack(new_dk, axis=1), jnp.stack(new_dv, axis=1),
            jnp.stack(new_gk, axis=1), jnp.stack(new_gv, axis=1))
```

```python
import functools
import math

import jax
import jax.numpy as jnp
import numpy as np
from jax import lax
from jax.experimental import pallas as pl
from jax.experimental.pallas import tpu as pltpu

F32 = jnp.float32
BF16 = jnp.bfloat16

D = 2048
BATCH, SEQ = 32, 256
DEPTH = 2
DEC_BATCH, DEC_SEQ = 2, 1024
PAST = 256
GRID_W = 64
ROPE_THETA = 10000.0
EPS = 1e-6
N_MOD = 6
HY_CH = 512
HY_BANDS = 16
HY_FFN = 64
DF_HEADS, DF_QK, DF_V = 4, 64, 128
GQ_HEADS, GQ_KV, GQ_DIM = 8, 2, 128
N_EXPERTS, TOP_K = 32, 4
D_FF = D
SWIGLU_LIMIT = 7.0
SWIGLU_ALPHA = 1.702

T_CTX = BATCH * SEQ
T_LAT = DEC_BATCH * DEC_SEQ
T = T_CTX + T_LAT
N_ASSIGN = T * TOP_K

C_HY = 0
C_DQ = 3 * HY_CH
C_DK = C_DQ + 512
C_DV = C_DK + 512
C_GQ = C_DV + 512
C_GK = C_GQ + 1024
C_GV = C_GK + 256
C_GATE = C_GV + 256
D_IN = C_GATE + 3 * D

LANE = 128
VMEM_LIMIT = 56 * 1024 * 1024

TM_TOK = 512
TM_PROJ = 2048
TN_PROJ = 512
QC = 256
TM_MOE = 512
TN_GU = 512
TM_DN = 256
SLAB = 16
TM_ROUTE = 256
TM_GATHER = 512
NT_MOE = N_ASSIGN // TM_MOE + N_EXPERTS
NPAD = NT_MOE * TM_MOE


def _params(*sem):
    return pltpu.CompilerParams(dimension_semantics=sem, vmem_limit_bytes=VMEM_LIMIT)


def _mod_row(tile_idx, tm):
    tok0 = tile_idx * tm
    return jnp.where(tok0 < T_CTX, 0, 1 + (tok0 - T_CTX) // DEC_SEQ)


def _adaln_kernel(c_ref, w_ref, b_ref, o_ref):
    c = c_ref[...]
    a = (c * jax.nn.sigmoid(c)).astype(BF16)
    o_ref[...] = jnp.dot(a, w_ref[...].astype(BF16), preferred_element_type=F32) + b_ref[...]


def adaln_all(cond8, w_mod, b_mod):
    tn = 1024
    return pl.pallas_call(
        _adaln_kernel,
        grid=(DEPTH, N_MOD * D // tn),
        in_specs=[pl.BlockSpec((8, D), lambda l, j: (0, 0)),
                  pl.BlockSpec((None, D, tn), lambda l, j: (l, 0, j)),
                  pl.BlockSpec((None, 1, tn), lambda l, j: (l, 0, j))],
        out_specs=pl.BlockSpec((None, 8, tn), lambda l, j: (l, 0, j)),
        out_shape=jax.ShapeDtypeStruct((DEPTH, 8, N_MOD * D), F32),
        compiler_params=_params("arbitrary", "arbitrary"),
        name="adaln",
    )(cond8, w_mod, b_mod.reshape(DEPTH, 1, N_MOD * D))


def _modulated_norm(x, g, sh, sc):
    ms = jnp.mean(x * x, axis=-1, keepdims=True)
    return (x * lax.rsqrt(ms + EPS) * g) * (1.0 + sc) + sh


def _modulate_kernel(x_ref, g_ref, mod_ref, o_ref, *, k_shift):
    r = _mod_row(pl.program_id(0), TM_TOK)
    sh = mod_ref[pl.ds(r, 1), k_shift * D:(k_shift + 1) * D]
    sc = mod_ref[pl.ds(r, 1), (k_shift + 1) * D:(k_shift + 2) * D]
    o_ref[...] = _modulated_norm(x_ref[...], g_ref[...], sh, sc).astype(BF16)


def modulate(x, g, mod, k_shift):
    return pl.pallas_call(
        functools.partial(_modulate_kernel, k_shift=k_shift),
        grid=(T // TM_TOK,),
        in_specs=[pl.BlockSpec((TM_TOK, D), lambda i: (i, 0)),
                  pl.BlockSpec((1, D), lambda i: (0, 0)),
                  pl.BlockSpec((8, N_MOD * D), lambda i: (0, 0))],
        out_specs=pl.BlockSpec((TM_TOK, D), lambda i: (i, 0)),
        out_shape=jax.ShapeDtypeStruct((T, D), BF16),
        compiler_params=_params("arbitrary"),
        name="modulate",
    )(x, g.reshape(1, D), mod)


def _proj_kernel(x_ref, w_ref, o_ref, wbf_ref, *, act):
    @pl.when(pl.program_id(1) == 0)
    def _():
        wbf_ref[...] = w_ref[...].astype(BF16)

    y = jnp.dot(x_ref[...], wbf_ref[...], preferred_element_type=F32)
    if act == "sigmoid":
        y = jax.nn.sigmoid(y)
    o_ref[...] = y.astype(o_ref.dtype)


def proj(h, w, layer, col0, ncols, out_dtype, act, name):
    blk0 = col0 // TN_PROJ
    return pl.pallas_call(
        functools.partial(_proj_kernel, act=act),
        grid=(ncols // TN_PROJ, T // TM_PROJ),
        in_specs=[pl.BlockSpec((TM_PROJ, D), lambda j, i: (i, 0)),
                  pl.BlockSpec((None, D, TN_PROJ), lambda j, i: (layer, 0, blk0 + j))],
        out_specs=pl.BlockSpec((TM_PROJ, TN_PROJ), lambda j, i: (i, j)),
        out_shape=jax.ShapeDtypeStruct((T, ncols), out_dtype),
        scratch_shapes=[pltpu.VMEM((D, TN_PROJ), BF16)],
        compiler_params=_params("arbitrary", "arbitrary"),
        name=name,
    )(h, w)


def _norm_halves(x, g):
    lo = lax.broadcasted_iota(jnp.int32, x.shape, 1) < DF_QK
    x2 = x * x
    s_lo = jnp.sum(jnp.where(lo, x2, 0.0), axis=-1, keepdims=True)
    s_hi = jnp.sum(jnp.where(lo, 0.0, x2), axis=-1, keepdims=True)
    inv = jnp.where(lo, lax.rsqrt(s_lo * (1.0 / DF_QK) + EPS), lax.rsqrt(s_hi * (1.0 / DF_QK) + EPS))
    return x * inv * g


def _norm_full(x, g):
    ms = jnp.mean(x * x, axis=-1, keepdims=True)
    return x * lax.rsqrt(ms + EPS) * g


def _rope(x, cos, sin_a, sin_b, shift):
    return x * cos + pltpu.roll(x, LANE - shift, 1) * sin_a + pltpu.roll(x, shift, 1) * sin_b


def _softmax(s):
    m = jnp.max(s, axis=-1, keepdims=True)
    e = jnp.exp(s - m)
    return e / jnp.sum(e, axis=-1, keepdims=True)


def _dot_nt(a, b):
    return lax.dot_general(a, b, (((1,), (1,)), ((), ())), preferred_element_type=F32)


def _attn_kernel(*refs, lq, has_ctx, out_scale):
    if has_ctx:
        (dq_ref, dk_ref, dv_ref, gq_ref, gk_ref, gv_ref, cdk_ref, cdv_ref, cgk_ref, cgv_ref,
         cd_ref, sad_ref, sbd_ref, cg_ref, sag_ref, sbg_ref, cdq_ref, sadq_ref, sbdq_ref,
         cgq_ref, sagq_ref, sbgq_ref,
         qn_ref, kn_ref, gqn_ref, gkn_ref, sub_ref, lam_ref,
         yb_ref, yc_ref, kd_s, vd_s, kg_s, vg_s) = refs
    else:
        (dq_ref, dk_ref, dv_ref, gq_ref, gk_ref, gv_ref,
         qn_ref, kn_ref, gqn_ref, gkn_ref, sub_ref, lam_ref,
         yb_ref, yc_ref, ndk_ref, ndv_ref, ngk_ref, ngv_ref, kd_s, vd_s, kg_s, vg_s) = refs

    @pl.when(pl.program_id(1) == 0)
    def _():
        for h in range(DF_HEADS):
            cs = slice(h * LANE, (h + 1) * LANE)
            kn = _norm_halves(dk_ref[:, cs], kn_ref[...])
            if has_ctx:
                kn = _rope(kn, cd_ref[...], sad_ref[...], sbd_ref[...], DF_QK // 4)
            else:
                ndk_ref[:, cs] = kn
            kd_s[0:lq, cs] = kn.astype(BF16)
        vd_s[0:lq, :] = dv_ref[...].astype(BF16)
        for g in range(GQ_KV):
            cs = slice(g * LANE, (g + 1) * LANE)
            kn = _norm_full(gk_ref[:, cs], gkn_ref[...])
            if has_ctx:
                kn = _rope(kn, cg_ref[...], sag_ref[...], sbg_ref[...], GQ_DIM // 4)
            else:
                ngk_ref[:, cs] = kn
            kg_s[0:lq, cs] = kn.astype(BF16)
        vg_s[0:lq, :] = gv_ref[...].astype(BF16)
        if has_ctx:
            kd_s[lq:lq + PAST, :] = cdk_ref[...].astype(BF16)
            vd_s[lq:lq + PAST, :] = cdv_ref[...].astype(BF16)
            kg_s[lq:lq + PAST, :] = cgk_ref[...].astype(BF16)
            vg_s[lq:lq + PAST, :] = cgv_ref[...].astype(BF16)
        else:
            ndv_ref[...] = dv_ref[...]
            ngv_ref[...] = gv_ref[...]

    lam = lam_ref[:, 0:1]
    lo = lax.broadcasted_iota(jnp.int32, (QC, LANE), 1) < DF_QK

    for h in range(DF_HEADS):
        cs = slice(h * LANE, (h + 1) * LANE)
        qn = _norm_halves(dq_ref[:, cs], qn_ref[...])
        if has_ctx:
            qn = _rope(qn, cdq_ref[...], sadq_ref[...], sbdq_ref[...], DF_QK // 4)
        qn = qn * (DF_QK ** -0.5)
        kh = kd_s[:, cs]
        p0 = _softmax(_dot_nt(jnp.where(lo, qn, 0.0).astype(BF16), kh))
        p1 = _softmax(_dot_nt(jnp.where(lo, 0.0, qn).astype(BF16), kh))
        w = (p0 - lam * p1).astype(BF16)
        o = jnp.dot(w, vd_s[:, cs], preferred_element_type=F32)
        yb_ref[:, cs] = (_norm_full(o, sub_ref[...]) * out_scale).astype(yb_ref.dtype)

    for hq in range(GQ_HEADS):
        g = hq // (GQ_HEADS // GQ_KV)
        cs = slice(hq * LANE, (hq + 1) * LANE)
        ks = slice(g * LANE, (g + 1) * LANE)
        qn = _norm_full(gq_ref[:, cs], gqn_ref[...])
        if has_ctx:
            qn = _rope(qn, cgq_ref[...], sagq_ref[...], sbgq_ref[...], GQ_DIM // 4)
        s = _dot_nt(qn.astype(BF16), kg_s[:, ks]) * (GQ_DIM ** -0.5)
        p = _softmax(s).astype(BF16)
        yc_ref[:, cs] = jnp.dot(p, vg_s[:, ks], preferred_element_type=F32).astype(yc_ref.dtype)


def _rope_tables(d_head):
    t = np.arange(DEC_SEQ)
    row, col = (t // GRID_W).astype(np.float64), (t % GRID_W).astype(np.float64)
    q = d_head // 4
    inv = ROPE_THETA ** (-np.arange(q, dtype=np.float64) / q)
    ar, ac = row[:, None] * inv, col[:, None] * inv
    z = np.zeros_like(ar)
    cos = np.concatenate([np.cos(ar), np.cos(ar), np.cos(ac), np.cos(ac)], -1)
    sin_a = np.concatenate([-np.sin(ar), z, -np.sin(ac), z], -1)
    sin_b = np.concatenate([z, np.sin(ar), z, np.sin(ac)], -1)
    rep = LANE // d_head
    return tuple(jnp.asarray(np.tile(a, (1, rep)), F32) for a in (cos, sin_a, sin_b))


def attention(p1, gains, lam, layer_idx, caches=None):
    has_ctx = caches is not None
    lq = DEC_SEQ if has_ctx else SEQ
    n_seq = DEC_BATCH if has_ctx else BATCH
    row0 = (T_CTX // lq) if has_ctx else 0
    lk = lq + (PAST if has_ctx else 0)
    nq = lq // QC
    lam_init = 0.8 - 0.6 * math.exp(-0.3 * layer_idx)

    def qspec(width, col0):
        return pl.BlockSpec((QC, width), lambda b, q: ((row0 + b) * nq + q, col0 // width))

    def kspec(width, col0):
        return pl.BlockSpec((lq, width), lambda b, q: (row0 + b, col0 // width))

    const = lambda shape: pl.BlockSpec(shape, lambda b, q: (0,) * len(shape))
    in_specs = [qspec(512, C_DQ), kspec(512, C_DK), kspec(512, C_DV),
                qspec(1024, C_GQ), kspec(256, C_GK), kspec(256, C_GV)]
    args = [p1] * 6
    if has_ctx:
        cdk, cdv, cgk, cgv = caches
        in_specs += [pl.BlockSpec((None, PAST, 512), lambda b, q: (b, 0, 0)),
                     pl.BlockSpec((None, PAST, 512), lambda b, q: (b, 0, 0)),
                     pl.BlockSpec((None, PAST, 256), lambda b, q: (b, 0, 0)),
                     pl.BlockSpec((None, PAST, 256), lambda b, q: (b, 0, 0))]
        args += [cdk, cdv, cgk, cgv]
        td, tg = _rope_tables(DF_QK), _rope_tables(GQ_DIM)
        in_specs += [const((lq, LANE))] * 6 + [pl.BlockSpec((QC, LANE), lambda b, q: (q, 0))] * 6
        args += list(td) + list(tg) + list(td) + list(tg)
    in_specs += [const((1, LANE))] * 6
    args += list(gains) + [jnp.full((1, LANE), lam, F32)]

    out_specs = [pl.BlockSpec((QC, 512), lambda b, q: ((row0 + b) * nq + q - row0 * nq, 0)),
                 pl.BlockSpec((QC, 1024), lambda b, q: ((row0 + b) * nq + q - row0 * nq, 0))]
    n_rows = n_seq * lq
    out_shape = [jax.ShapeDtypeStruct((n_rows, 512), BF16), jax.ShapeDtypeStruct((n_rows, 1024), BF16)]
    if not has_ctx:
        out_specs += [pl.BlockSpec((lq, 512), lambda b, q: (b, 0)), pl.BlockSpec((lq, 512), lambda b, q: (b, 0)),
                      pl.BlockSpec((lq, 256), lambda b, q: (b, 0)), pl.BlockSpec((lq, 256), lambda b, q: (b, 0))]
        out_shape += [jax.ShapeDtypeStruct((n_rows, 512), F32), jax.ShapeDtypeStruct((n_rows, 512), F32),
                      jax.ShapeDtypeStruct((n_rows, 256), F32), jax.ShapeDtypeStruct((n_rows, 256), F32)]
    return pl.pallas_call(
        functools.partial(_attn_kernel, lq=lq, has_ctx=has_ctx, out_scale=1.0 - lam_init),
        grid=(n_seq, nq),
        in_specs=in_specs,
        out_specs=out_specs,
        out_shape=out_shape,
        scratch_shapes=[pltpu.VMEM((lk, 512), BF16), pltpu.VMEM((lk, 512), BF16),
                        pltpu.VMEM((lk, 256), BF16), pltpu.VMEM((lk, 256), BF16)],
        compiler_params=_params("arbitrary", "arbitrary"),
        name="attn_latent" if has_ctx else "attn_context",
    )(*args)


def _short_conv(u, w_ref, b_ref, length):
    row = lax.broadcasted_iota(jnp.int32, u.shape, 0)
    prev = jnp.where(row == 0, 0.0, pltpu.roll(u, 1, 0))
    nxt = jnp.where(row == length - 1, 0.0, pltpu.roll(u, length - 1, 0))
    return w_ref[0:1, :] * prev + w_ref[1:2, :] * u + w_ref[2:3, :] * nxt + b_ref[...]


def _hyena_kernel(v_ref, x_ref, cwv_ref, cbv_ref, cwx_ref, cbx_ref, skip_ref, fm_ref, gm_ref,
                  a_ref, b_ref, a2_ref, o_ref, z_s, *, length):
    n = pl.program_id(2)

    @pl.when(n == 0)
    def _():
        z_s[...] = _short_conv(v_ref[...], cwv_ref, cbv_ref, length)

    z = z_s[...]
    xn = _short_conv(x_ref[...], cwx_ref, cbx_ref, length)
    u = jnp.dot(fm_ref[...], z.astype(BF16), preferred_element_type=F32)
    ur, ui = u[:length], u[length:]
    a, b, a2 = a_ref[...], b_ref[...], a2_ref[...]
    yr = (ur * a - ui * b).astype(BF16)
    yi = (ur * b + ui * a2).astype(BF16)
    conv = (jnp.dot(gm_ref[:, :length], yr, preferred_element_type=F32)
            + jnp.dot(gm_ref[:, length:], yi, preferred_element_type=F32))
    z = xn * (conv + skip_ref[...] * z)
    z_s[...] = z
    o_ref[...] = z.astype(o_ref.dtype)


def _dft_matrices(length):
    n = 2 * length
    f = np.arange(length)[:, None]
    t = np.arange(length)[None, :]
    ang = ((f * t) % n).astype(np.float64) * (2.0 * math.pi / n)
    cos, sin = np.cos(ang), np.sin(ang)
    alt = np.broadcast_to(np.where(t % 2 == 0, 1.0, -1.0), sin.shape)
    fm = np.concatenate([cos, np.where(f == 0, alt, -sin)], axis=0)
    wre = np.where(f == 0, 1.0 / n, 2.0 / n) * cos
    wim = np.where(f == 0, alt / n, -(2.0 / n) * sin)
    gm = np.concatenate([wre, wim], axis=0).T
    fm32, gm32 = jnp.asarray(fm, F32), jnp.asarray(gm, F32)
    fm_hi, fm_lo = _split3(fm32)
    return fm_hi, fm_lo, gm32.astype(BF16)


def _dot3(a_hi, a_lo, b):
    b_hi, b_lo = _split3(b)
    return (jnp.dot(a_hi, b_hi, preferred_element_type=F32) + jnp.dot(a_hi, b_lo, preferred_element_type=F32)
            + jnp.dot(a_lo, b_hi, preferred_element_type=F32))


def _spectrum_kernel(kp_ref, km_ref, fhi_ref, flo_ref, a_ref, b_ref, a2_ref, *, length):
    p = _dot3(fhi_ref[...], flo_ref[...], kp_ref[...])
    q = _dot3(fhi_ref[...], flo_ref[...], km_ref[...])
    first = lax.broadcasted_iota(jnp.int32, (length, p.shape[1]), 0) == 0
    a = p[:length]
    a_ref[...] = a
    b_ref[...] = jnp.where(first, 0.0, q[length:])
    a2_ref[...] = jnp.where(first, p[length:length + 1], a)


def _hyena_filter_spectra(length, dft, w1, b1, w2, b2, w3, freq):
    hp = lax.Precision.HIGHEST
    t = jnp.linspace(0.0, 1.0, length, dtype=F32)[:, None]
    bands = jnp.linspace(1e-4, HY_BANDS - 1, HY_BANDS, dtype=F32)
    wpos = (2 * math.pi / length) * jnp.arange(length, dtype=F32)[:, None] * bands
    z = jnp.concatenate([t, jnp.cos(wpos), -jnp.sin(wpos)], -1)
    h = jnp.sin(freq * (jnp.dot(z, w1, precision=hp) + b1))
    h = jnp.sin(freq * (jnp.dot(h, w2, precision=hp) + b2))
    h = jnp.dot(h, w3, precision=hp).reshape(length, 2, 2, HY_CH)
    max_decay = math.log(1e-2) / 0.3
    min_decay = math.log(1e-2) / 1.5
    deltas = jnp.linspace(min_decay, max_decay, HY_CH, dtype=F32)
    h = h * jnp.exp(-t[:, :, None, None] * jnp.abs(deltas))
    fwd = h[:, :, 0]
    bwd = jnp.where(jnp.arange(length)[:, None, None] == 0, 0.0, h[:, :, 1])
    kp = (fwd + bwd).reshape(length, 2 * HY_CH)
    km = (fwd - bwd).reshape(length, 2 * HY_CH)
    fm_hi, fm_lo, _ = dft
    cw = 256
    ncb = HY_CH // cw
    in_blk = pl.BlockSpec((length, cw), lambda n, c: (0, n * ncb + c))
    const = pl.BlockSpec((2 * length, length), lambda n, c: (0, 0))
    out_blk = pl.BlockSpec((None, length, cw), lambda n, c: (n, 0, c))
    shape = jax.ShapeDtypeStruct((2, length, HY_CH), F32)
    return pl.pallas_call(
        functools.partial(_spectrum_kernel, length=length),
        grid=(2, ncb),
        in_specs=[in_blk, in_blk, const, const],
        out_specs=[out_blk, out_blk, out_blk],
        out_shape=[shape, shape, shape],
        compiler_params=_params("arbitrary", "arbitrary"),
        name=f"hyena_spectrum_{length}",
    )(kp, km, fm_hi, fm_lo)


def hyena(p1, conv_w, conv_b, skip, spectra, dft, length, n_seq, row0, cw):
    fm, _, gm = dft
    a, b, a2 = spectra
    ncb = HY_CH // cw
    vblk = 2 * ncb
    const = lambda shape: pl.BlockSpec(shape, lambda s, c, n: (0,) * len(shape))
    spec_spec = pl.BlockSpec((None, length, cw), lambda s, c, n: (n, 0, c))
    return pl.pallas_call(
        functools.partial(_hyena_kernel, length=length),
        grid=(n_seq, ncb, 2),
        in_specs=[pl.BlockSpec((length, cw), lambda s, c, n: (row0 + s, vblk + c)),
                  pl.BlockSpec((length, cw), lambda s, c, n: (row0 + s, n * ncb + c)),
                  pl.BlockSpec((3, cw), lambda s, c, n: (0, vblk + c)),
                  pl.BlockSpec((1, cw), lambda s, c, n: (0, vblk + c)),
                  pl.BlockSpec((3, cw), lambda s, c, n: (0, n * ncb + c)),
                  pl.BlockSpec((1, cw), lambda s, c, n: (0, n * ncb + c)),
                  pl.BlockSpec((None, 1, cw), lambda s, c, n: (n, 0, c)),
                  const((2 * length, length)), const((length, 2 * length)),
                  spec_spec, spec_spec, spec_spec],
        out_specs=pl.BlockSpec((length, cw), lambda s, c, n: (s, c)),
        out_shape=jax.ShapeDtypeStruct((n_seq * length, HY_CH), BF16),
        scratch_shapes=[pltpu.VMEM((length, cw), F32)],
        compiler_params=_params("arbitrary", "arbitrary", "arbitrary"),
        name=f"hyena_{length}",
    )(p1, p1, conv_w, conv_b.reshape(1, -1), conv_w, conv_b.reshape(1, -1),
      skip.reshape(2, 1, HY_CH), fm, gm, a, b, a2)


def _merge_kernel(ya_ref, yb_ref, yc_ref, g_ref, wa_ref, wb_ref, wc_ref, o_ref):
    a = jnp.dot(ya_ref[...], wa_ref[...], preferred_element_type=F32)
    b = jnp.dot(yb_ref[...], wb_ref[...], preferred_element_type=F32)
    c = jnp.dot(yc_ref[...], wc_ref[...], preferred_element_type=F32)
    m = g_ref[:, 0:D] * a + g_ref[:, D:2 * D] * b + g_ref[:, 2 * D:3 * D] * c
    o_ref[...] = m.astype(o_ref.dtype)


def merge(ya, yb, yc, gates, wa, wb, wc):
    tm = 256
    const = lambda shape: pl.BlockSpec(shape, lambda i: (0, 0))
    return pl.pallas_call(
        _merge_kernel,
        grid=(T // tm,),
        in_specs=[pl.BlockSpec((tm, 512), lambda i: (i, 0)), pl.BlockSpec((tm, 512), lambda i: (i, 0)),
                  pl.BlockSpec((tm, 1024), lambda i: (i, 0)), pl.BlockSpec((tm, 3 * D), lambda i: (i, 0)),
                  const((512, D)), const((512, D)), const((1024, D))],
        out_specs=pl.BlockSpec((tm, D), lambda i: (i, 0)),
        out_shape=jax.ShapeDtypeStruct((T, D), BF16),
        compiler_params=_params("arbitrary"),
        name="merge",
    )(ya, yb, yc, gates, wa, wb, wc)


def _split3(x):
    hi = x.astype(BF16)
    return hi, (x - hi.astype(F32)).astype(BF16)


def _out_router_kernel(m_ref, x_ref, wout_ref, mod_ref, g_ref, rw_ref, rb_ref,
                       x1_ref, h2_ref, route_ref, wt_ref, cnt_ref, cnt_s, *, tm):
    r = _mod_row(pl.program_id(0), tm)
    o = jnp.dot(m_ref[...], wout_ref[...], preferred_element_type=F32)
    x1 = x_ref[...] + mod_ref[pl.ds(r, 1), 2 * D:3 * D] * o
    x1_ref[...] = x1
    h2 = _modulated_norm(x1, g_ref[...], mod_ref[pl.ds(r, 1), 3 * D:4 * D], mod_ref[pl.ds(r, 1), 4 * D:5 * D])
    for s in range(SLAB):
        h2_ref[pl.ds(s, tm, stride=SLAB), :] = h2[:, s * LANE:(s + 1) * LANE]

    h_hi, h_lo = _split3(h2)
    w_hi, w_lo = _split3(rw_ref[...])
    logits = (jnp.dot(h_hi, w_hi, preferred_element_type=F32) + jnp.dot(h_hi, w_lo, preferred_element_type=F32)
              + jnp.dot(h_lo, w_hi, preferred_element_type=F32)) + rb_ref[...]

    lane = lax.broadcasted_iota(jnp.int32, logits.shape, 1)
    vals, idxs = [], []
    for _ in range(TOP_K):
        mx = jnp.max(logits, axis=-1, keepdims=True)
        ix = jnp.min(jnp.where(logits == mx, lane, LANE), axis=-1, keepdims=True)
        vals.append(mx)
        idxs.append(ix)
        logits = jnp.where(lane == ix, -jnp.inf, logits)
    es = [jnp.exp(v - vals[0]) for v in vals]
    inv = 1.0 / (es[0] + es[1] + es[2] + es[3])

    @pl.when(pl.program_id(0) == 0)
    def _():
        cnt_s[...] = jnp.zeros_like(cnt_s)

    chosen = jnp.zeros(lane.shape, F32)
    for k in range(TOP_K):
        chosen = jnp.where(lane == idxs[k], 1.0, chosen)
    earlier = (lax.broadcasted_iota(jnp.int32, (tm, tm), 0) > lax.broadcasted_iota(jnp.int32, (tm, tm), 1))
    before = jnp.dot(earlier.astype(BF16), chosen.astype(BF16), preferred_element_type=F32) + cnt_s[...]
    route_out = jnp.zeros(lane.shape, jnp.int32)
    wt_out = jnp.zeros(lane.shape, F32)
    for k in range(TOP_K):
        rank = jnp.sum(jnp.where(lane == idxs[k], before, 0.0), axis=-1, keepdims=True).astype(jnp.int32)
        route_out = jnp.where(lane == k, idxs[k], route_out)
        route_out = jnp.where(lane == TOP_K + k, rank, route_out)
        wt_out = jnp.where(lane == k, es[k] * inv, wt_out)
    route_ref[...] = route_out
    wt_ref[...] = wt_out
    total = cnt_s[...] + jnp.sum(chosen, axis=0, keepdims=True)
    cnt_s[...] = total
    cnt_ref[...] = total.astype(jnp.int32)


def out_router(merged, x, w_out, mod, norm2_g, router_w, router_b):
    tm = TM_ROUTE
    const = lambda shape: pl.BlockSpec(shape, lambda i: (0, 0))
    rw = jnp.zeros((D, LANE), F32).at[:, :N_EXPERTS].set(router_w)
    rb = jnp.full((1, LANE), -1e30, F32).at[0, :N_EXPERTS].set(router_b)
    return pl.pallas_call(
        functools.partial(_out_router_kernel, tm=tm),
        grid=(T // tm,),
        in_specs=[pl.BlockSpec((tm, D), lambda i: (i, 0)), pl.BlockSpec((tm, D), lambda i: (i, 0)),
                  const((D, D)), const((8, N_MOD * D)), const((1, D)), const((D, LANE)), const((1, LANE))],
        out_specs=[pl.BlockSpec((tm, D), lambda i: (i, 0)), pl.BlockSpec((tm * SLAB, LANE), lambda i: (i, 0)),
                   pl.BlockSpec((tm, LANE), lambda i: (i, 0)), pl.BlockSpec((tm, LANE), lambda i: (i, 0)),
                   const((1, LANE))],
        out_shape=[jax.ShapeDtypeStruct((T, D), F32), jax.ShapeDtypeStruct((T * SLAB, LANE), F32),
                   jax.ShapeDtypeStruct((T, LANE), jnp.int32), jax.ShapeDtypeStruct((T, LANE), F32),
                   jax.ShapeDtypeStruct((1, LANE), jnp.int32)],
        scratch_shapes=[pltpu.VMEM((1, LANE), F32)],
        compiler_params=_params("arbitrary"),
        name="out_router",
    )(merged, x, w_out, mod, norm2_g.reshape(1, D), rw, rb)


def _segments(counts):
    padded = ((counts + TM_MOE - 1) // TM_MOE) * TM_MOE
    seg_end = jnp.cumsum(padded)
    return padded, seg_end - padded, seg_end


def _step_plan(counts, n_col_blocks, tm):
    padded, seg_start, seg_end = _segments(counts)
    n_tiles = seg_end[-1] // tm
    step = jnp.arange((NPAD // tm) * n_col_blocks, dtype=jnp.int32)
    tile_probe = jnp.minimum(step // n_col_blocks, n_tiles - 1)
    ends_before = (seg_end[None, :] <= (tile_probe * tm)[:, None]).astype(jnp.int32)
    ex = jnp.minimum(jnp.sum(ends_before, axis=1), N_EXPERTS - 1)
    onehot = (ex[:, None] == jnp.arange(N_EXPERTS, dtype=jnp.int32)[None, :]).astype(jnp.int32)
    t0 = jnp.sum(onehot * seg_start[None, :], axis=1) // tm
    ne = jnp.maximum(jnp.sum(onehot * padded[None, :], axis=1) // tm, 1)
    valid = step < n_tiles * n_col_blocks
    local = jnp.where(valid, step - n_col_blocks * t0, n_col_blocks * ne - 1)
    col = local // ne
    tile = t0 + local % ne
    first = jnp.logical_and(valid, local % ne == 0)
    rows = jnp.clip(jnp.sum(onehot * counts[None, :], axis=1) - (tile - t0) * tm, 0, tm)
    last_col = col == n_col_blocks - 1
    ends_before_next = (seg_end[None, :] <= ((t0 + ne) * tm)[:, None]).astype(jnp.int32)
    next_ex = jnp.where(last_col, jnp.minimum(jnp.sum(ends_before_next, axis=1), N_EXPERTS - 1), ex)
    next_col = jnp.where(last_col, 0, col + 1)
    next_ok = jnp.logical_and(first, jnp.logical_or(~last_col, t0 + ne < n_tiles))
    lower = (jnp.arange(N_EXPERTS, dtype=jnp.int32)[None, :] < ex[:, None]).astype(jnp.int32)
    blocks_before = n_col_blocks * jnp.sum(lower * (padded > 0).astype(jnp.int32)[None, :], axis=1) + col
    i32 = lambda a: a.astype(jnp.int32)
    return dict(ex=i32(ex), col=i32(col), tile=i32(tile), first=i32(first), valid=i32(valid), rows=i32(rows),
                next_ex=i32(next_ex), next_col=i32(next_col), next_ok=i32(next_ok), slot=i32(blocks_before % 2))


def _dispatch_kernel(pos_ref, h_ref, xs_hbm, sem):
    def issue(t, carry):
        src = pl.multiple_of(t * SLAB, SLAB)
        for k in range(TOP_K):
            dst = pl.multiple_of(pos_ref[0, t * TOP_K + k] * SLAB, SLAB)
            pltpu.make_async_copy(h_ref.at[pl.ds(src, SLAB), :], xs_hbm.at[pl.ds(dst, SLAB), :],
                                  sem).start(priority=k % 2)
        return carry

    lax.fori_loop(0, TM_GATHER, issue, 0, unroll=2)
    for _ in range(TOP_K):
        pltpu.make_async_copy(h_ref, xs_hbm.at[pl.ds(0, TM_GATHER * SLAB), :], sem).wait()


def dispatch(h2_slab, pos):
    n_steps = T // TM_GATHER
    return pl.pallas_call(
        _dispatch_kernel,
        grid=(n_steps,),
        in_specs=[pl.BlockSpec((None, 1, TM_GATHER * TOP_K), lambda i: (i, 0, 0), memory_space=pltpu.SMEM),
                  pl.BlockSpec((TM_GATHER * SLAB, LANE), lambda i: (i, 0))],
        out_specs=pl.BlockSpec(memory_space=pl.ANY),
        out_shape=jax.ShapeDtypeStruct((NPAD * SLAB, LANE), F32),
        scratch_shapes=[pltpu.SemaphoreType.DMA(())],
        compiler_params=_params("arbitrary"),
        name="moe_dispatch",
    )(pos.reshape(n_steps, 1, TM_GATHER * TOP_K), h2_slab)


_PLAN_KEYS = ("ex", "col", "tile", "first", "valid", "rows", "next_ex", "next_col", "next_ok", "slot")


def _stream_weights(s, plan, fetch, on_arrival):
    @pl.when(s == 0)
    def _():
        for cp in fetch(plan["ex"][0], plan["col"][0], 0):
            cp.start()

    @pl.when(plan["first"][s] == 1)
    def _():
        slot = plan["slot"][s]
        for cp in fetch(plan["ex"][s], plan["col"][s], slot):
            cp.wait()

        @pl.when(plan["next_ok"][s] == 1)
        def _():
            for cp in fetch(plan["next_ex"][s], plan["next_col"][s], 1 - slot):
                cp.start()

        on_arrival(slot)


def _gate_up_kernel(*refs, layer):
    plan = dict(zip(_PLAN_KEYS, refs[:len(_PLAN_KEYS)]))
    x_ref, w_hbm, bg_ref, bu_ref, o_ref, stage, wg_s, wu_s, sem = refs[len(_PLAN_KEYS):]
    s = pl.program_id(0)

    def fetch(e, c, slot):
        c0 = pl.multiple_of(c * TN_GU, TN_GU)
        return [pltpu.make_async_copy(w_hbm.at[layer, e, :, pl.ds(c0, TN_GU)], stage.at[slot, 0], sem.at[slot]),
                pltpu.make_async_copy(w_hbm.at[layer, e, :, pl.ds(D_FF + c0, TN_GU)], stage.at[slot, 1], sem.at[slot])]

    def on_arrival(slot):
        wg_s[...] = stage[slot, 0].astype(BF16)
        wu_s[...] = stage[slot, 1].astype(BF16)

    _stream_weights(s, plan, fetch, on_arrival)

    def compute(rows):
        g = jnp.broadcast_to(bg_ref[...], (rows, TN_GU))
        u = jnp.broadcast_to(bu_ref[...], (rows, TN_GU))
        for c in range(SLAB // 2):
            xc = jnp.concatenate([x_ref[pl.ds(2 * c, rows, stride=SLAB), :],
                                  x_ref[pl.ds(2 * c + 1, rows, stride=SLAB), :]], axis=1).astype(BF16)
            ks = slice(c * 2 * LANE, (c + 1) * 2 * LANE)
            g = g + jnp.dot(xc, wg_s[ks, :], preferred_element_type=F32)
            u = u + jnp.dot(xc, wu_s[ks, :], preferred_element_type=F32)
        gate = jnp.minimum(g, SWIGLU_LIMIT)
        up = jnp.clip(u, -SWIGLU_LIMIT, SWIGLU_LIMIT)
        o_ref[0:rows, :] = (gate * jax.nn.sigmoid(SWIGLU_ALPHA * gate) * (up + 1.0)).astype(o_ref.dtype)

    live = plan["valid"][s] == 1
    half = plan["rows"][s] <= TM_MOE // 2

    @pl.when(jnp.logical_and(live, jnp.logical_not(half)))
    def _():
        compute(TM_MOE)

    @pl.when(jnp.logical_and(live, half))
    def _():
        compute(TM_MOE // 2)


def gate_up(x_sorted, w_gu, b_gu, layer, plan):
    ncb = D_FF // TN_GU
    n = len(_PLAN_KEYS)
    imap = lambda f: (lambda s, *p: f(s, dict(zip(_PLAN_KEYS, p))))
    grid_spec = pltpu.PrefetchScalarGridSpec(
        num_scalar_prefetch=n,
        grid=(NT_MOE * ncb,),
        in_specs=[pl.BlockSpec((TM_MOE * SLAB, LANE), imap(lambda s, p: (p["tile"][s], 0))),
                  pl.BlockSpec(memory_space=pl.ANY),
                  pl.BlockSpec((None, None, 1, TN_GU), imap(lambda s, p: (layer, p["ex"][s], 0, p["col"][s]))),
                  pl.BlockSpec((None, None, 1, TN_GU), imap(lambda s, p: (layer, p["ex"][s], 0, ncb + p["col"][s])))],
        out_specs=pl.BlockSpec((TM_MOE, TN_GU), imap(lambda s, p: (p["tile"][s], p["col"][s]))),
        scratch_shapes=[pltpu.VMEM((2, 2, D, TN_GU), F32), pltpu.VMEM((D, TN_GU), BF16), pltpu.VMEM((D, TN_GU), BF16),
                        pltpu.SemaphoreType.DMA((2,))],
    )
    b4 = b_gu.reshape(DEPTH, N_EXPERTS, 1, 2 * D_FF)
    return pl.pallas_call(
        functools.partial(_gate_up_kernel, layer=layer),
        grid_spec=grid_spec,
        out_shape=jax.ShapeDtypeStruct((NPAD, D_FF), BF16),
        compiler_params=_params("arbitrary"),
        name="moe_gate_up",
    )(*[plan[k] for k in _PLAN_KEYS], x_sorted, w_gu, b4, b4)


def _down_kernel(*refs, layer):
    plan = dict(zip(_PLAN_KEYS, refs[:len(_PLAN_KEYS)]))
    a_ref, w_hbm, b_ref, o_ref, stage, w_s, sem = refs[len(_PLAN_KEYS):]
    s = pl.program_id(0)

    def fetch(e, c, slot):
        return [pltpu.make_async_copy(w_hbm.at[layer, e], stage.at[slot], sem.at[slot])]

    def on_arrival(slot):
        w_s[...] = stage[slot].astype(BF16)

    _stream_weights(s, plan, fetch, on_arrival)

    @pl.when(jnp.logical_and(plan["valid"][s] == 1, plan["rows"][s] > 0))
    def _():
        y = jnp.dot(a_ref[...], w_s[...], preferred_element_type=F32) + b_ref[...]
        for r in range(SLAB):
            o_ref[pl.ds(r, TM_DN, stride=SLAB), :] = y[:, r * LANE:(r + 1) * LANE]


def down(act, w_dn, b_dn, layer, plan):
    n = len(_PLAN_KEYS)
    imap = lambda f: (lambda s, *p: f(s, dict(zip(_PLAN_KEYS, p))))
    grid_spec = pltpu.PrefetchScalarGridSpec(
        num_scalar_prefetch=n,
        grid=(NPAD // TM_DN,),
        in_specs=[pl.BlockSpec((TM_DN, D_FF), imap(lambda s, p: (p["tile"][s], 0))),
                  pl.BlockSpec(memory_space=pl.ANY),
                  pl.BlockSpec((None, None, 1, D), imap(lambda s, p: (layer, p["ex"][s], 0, 0)))],
        out_specs=pl.BlockSpec((TM_DN * SLAB, LANE), imap(lambda s, p: (p["tile"][s], 0))),
        scratch_shapes=[pltpu.VMEM((2, D_FF, D), F32), pltpu.VMEM((D_FF, D), BF16), pltpu.SemaphoreType.DMA((2,))],
    )
    return pl.pallas_call(
        functools.partial(_down_kernel, layer=layer),
        grid_spec=grid_spec,
        out_shape=jax.ShapeDtypeStruct((NPAD * SLAB, LANE), F32),
        compiler_params=_params("arbitrary"),
        name="moe_down",
    )(*[plan[k] for k in _PLAN_KEYS], act, w_dn, b_dn.reshape(DEPTH, N_EXPERTS, 1, D))


def _combine_kernel(pos_ref, ys_hbm, x_ref, wt_ref, mod_ref, o_ref, buf, sem):
    i = pl.program_id(0)
    tm = TM_GATHER

    def issue(t, carry):
        for k in range(TOP_K):
            src = pl.multiple_of(pos_ref[0, t * TOP_K + k] * SLAB, SLAB)
            dst = pl.multiple_of((k * tm + t) * SLAB, SLAB)
            pltpu.make_async_copy(ys_hbm.at[pl.ds(src, SLAB), :], buf.at[pl.ds(dst, SLAB), :],
                                  sem).start(priority=k % 2)
        return carry

    lax.fori_loop(0, tm, issue, 0, unroll=2)
    pltpu.make_async_copy(ys_hbm.at[pl.ds(0, TOP_K * tm * SLAB), :], buf, sem).wait()

    gate_row = mod_ref[pl.ds(_mod_row(i, tm), 1), 5 * D:6 * D]
    wk = [wt_ref[:, k:k + 1] for k in range(TOP_K)]
    for s in range(SLAB):
        acc = jnp.zeros((tm, LANE), F32)
        for k in range(TOP_K):
            acc = acc + wk[k] * buf[pl.ds(k * tm * SLAB + s, tm, stride=SLAB), :]
        cs = slice(s * LANE, (s + 1) * LANE)
        o_ref[:, cs] = x_ref[:, cs] + gate_row[:, cs] * acc


def combine(ys, pos, wts, x1, mod):
    n_steps = T // TM_GATHER
    return pl.pallas_call(
        _combine_kernel,
        grid=(n_steps,),
        in_specs=[pl.BlockSpec((None, 1, TM_GATHER * TOP_K), lambda i: (i, 0, 0), memory_space=pltpu.SMEM),
                  pl.BlockSpec(memory_space=pl.ANY),
                  pl.BlockSpec((TM_GATHER, D), lambda i: (i, 0)),
                  pl.BlockSpec((TM_GATHER, LANE), lambda i: (i, 0)),
                  pl.BlockSpec((8, N_MOD * D), lambda i: (0, 0))],
        out_specs=pl.BlockSpec((TM_GATHER, D), lambda i: (i, 0)),
        out_shape=jax.ShapeDtypeStruct((T, D), F32),
        scratch_shapes=[pltpu.VMEM((TOP_K * TM_GATHER * SLAB, LANE), F32), pltpu.SemaphoreType.DMA(())],
        compiler_params=_params("arbitrary"),
        name="moe_combine",
    )(pos.reshape(n_steps, 1, TM_GATHER * TOP_K), ys, x1, wts, mod)


def kernel(x_prompt, x_sample, cache_diff_k, cache_diff_v, cache_gqa_k, cache_gqa_v, c, c_ctx, w_mod, b_mod, norm1_g, norm2_g, w_in, hy_conv_w, hy_conv_b, hy_f_w1, hy_f_b1, hy_f_w2, hy_f_b2, hy_f_w3, hy_freq, hy_skip, df_qn_g, df_kn_g, df_lq1, df_lk1, df_lq2, df_lk2, df_subln_g, gq_qn_g, gq_kn_g, w_hy_o, w_df_o, w_gq_o, w_out, router_w, router_b, w_gu, b_gu, w_dn, b_dn):
    x = jnp.concatenate([x_prompt.reshape(T_CTX, D), x_sample.reshape(T_LAT, D)], axis=0)
    cond8 = jnp.zeros((8, D), F32).at[0].set(c_ctx).at[1:1 + DEC_BATCH].set(c)
    mods = adaln_all(cond8, w_mod, b_mod)

    cdk = cache_diff_k.reshape(DEC_BATCH, DEPTH, PAST, 512)
    cdv = cache_diff_v.reshape(DEC_BATCH, DEPTH, PAST, 512)
    cgk = cache_gqa_k.reshape(DEC_BATCH, DEPTH, PAST, 256)
    cgv = cache_gqa_v.reshape(DEC_BATCH, DEPTH, PAST, 256)

    dft_c, dft_l = _dft_matrices(SEQ), _dft_matrices(DEC_SEQ)

    new_dk, new_dv, new_gk, new_gv = [], [], [], []
    for l in range(DEPTH):
        mod = mods[l]
        h = modulate(x, norm1_g[l], mod, 0)
        p1 = proj(h, w_in, l, 0, C_GATE, F32, None, "proj_mix")
        gates = proj(h, w_in, l, C_GATE, 3 * D, BF16, "sigmoid", "proj_gates")

        tile128 = lambda g: jnp.tile(g, LANE // g.shape[0]).reshape(1, LANE)
        gains = (tile128(df_qn_g[l]), tile128(df_kn_g[l]), tile128(gq_qn_g[l]), tile128(gq_kn_g[l]),
                 tile128(df_subln_g[l]))
        lam_init = 0.8 - 0.6 * math.exp(-0.3 * l)
        lam = (jnp.exp(jnp.sum(df_lq1[l] * df_lk1[l])) - jnp.exp(jnp.sum(df_lq2[l] * df_lk2[l])) + lam_init)

        yb_c, yc_c, ndk, ndv, ngk, ngv = attention(p1, gains, lam, l)
        yb_l, yc_l = attention(p1, gains, lam, l, caches=(cdk[:, l], cdv[:, l], cgk[:, l], cgv[:, l]))
        new_dk.append(ndk.reshape(BATCH, SEQ, DF_HEADS, 2, DF_QK))
        new_dv.append(ndv.reshape(BATCH, SEQ, DF_HEADS, DF_V))
        new_gk.append(ngk.reshape(BATCH, SEQ, GQ_KV, GQ_DIM))
        new_gv.append(ngv.reshape(BATCH, SEQ, GQ_KV, GQ_DIM))

        filt = (hy_f_w1[l], hy_f_b1[l], hy_f_w2[l], hy_f_b2[l], hy_f_w3[l], hy_freq[l])
        ya_c = hyena(p1, hy_conv_w[l], hy_conv_b[l], hy_skip[l], _hyena_filter_spectra(SEQ, dft_c, *filt),
                     dft_c, SEQ, BATCH, 0, 512)
        ya_l = hyena(p1, hy_conv_w[l], hy_conv_b[l], hy_skip[l], _hyena_filter_spectra(DEC_SEQ, dft_l, *filt),
                     dft_l, DEC_SEQ, DEC_BATCH, T_CTX // DEC_SEQ, 256)

        ya = jnp.concatenate([ya_c, ya_l], axis=0)
        yb = jnp.concatenate([yb_c, yb_l], axis=0)
        yc = jnp.concatenate([yc_c, yc_l], axis=0)
        merged = merge(ya, yb, yc, gates, w_hy_o[l].astype(BF16), w_df_o[l].astype(BF16), w_gq_o[l].astype(BF16))
        x1, h2_slab, route, wts, cnt = out_router(merged, x, w_out[l].astype(BF16), mod, norm2_g[l],
                                                  router_w[l], router_b[l])

        counts = cnt[0, :N_EXPERTS]
        seg_start = _segments(counts)[1]
        pos = seg_start[route[:, :TOP_K]] + route[:, TOP_K:2 * TOP_K]
        x_sorted = dispatch(h2_slab, pos)
        act = gate_up(x_sorted, w_gu, b_gu, l, _step_plan(counts, D_FF // TN_GU, TM_MOE))
        ys = down(act, w_dn, b_dn, l, _step_plan(counts, 1, TM_DN))
        x = combine(ys, pos, wts, x1, mod)

    y_prompt = x[:T_CTX].reshape(BATCH, SEQ, D)
    y_sample = x[T_CTX:].reshape(DEC_BATCH, DEC_SEQ, D)
    return (y_prompt, y_sample, jnp.stack(new_dk, axis=1), jnp.stack(new_dv, axis=1),
            jnp.stack(new_gk, axis=1), jnp.stack(new_gv, axis=1))
```
